```python
import jax, jax.numpy as jnp
from jax import lax
import numpy as np

D_MODEL = 1024
BATCH = 8
SEQ = 4096
DEPTH = 2
DEC_BATCH = 8
DEC_SEQ = 16
PAST_LEN = 2048

CHUNK = 64
EPS = 1e-6
LRU_WIDTH = D_MODEL
LRU_BLOCKS = 8
LRU_BLOCK = LRU_WIDTH // LRU_BLOCKS
CONV_W = 4
LRU_C = 8.0
M_HEADS = 4
M_HEAD_DIM = D_MODEL // M_HEADS
M_WIDTH = M_HEADS * M_HEAD_DIM
A_HEAD_DIM = 64
A_HEADS = D_MODEL // A_HEAD_DIM
A_KV_HEADS = 2
A_GROUPS = A_HEADS // A_KV_HEADS
A_WIDTH = A_HEADS * A_HEAD_DIM
A_KV_WIDTH = A_KV_HEADS * A_HEAD_DIM
WINDOW = 128
ROPE_THETA = 10000.0
D_FF = 4 * D_MODEL
IN_SIZES = (LRU_WIDTH, LRU_WIDTH,
            M_WIDTH, M_WIDTH, M_WIDTH, M_WIDTH,
            M_HEADS, M_HEADS,
            A_WIDTH, A_KV_WIDTH, A_KV_WIDTH,
            D_MODEL, D_MODEL, D_MODEL)
IN_SPLITS = tuple(sum(IN_SIZES[:i + 1]) for i in range(len(IN_SIZES) - 1))
IN_COLS = sum(IN_SIZES)

kernel_name = "hybrid_lru_mlstm_swa_stream_step"


def rms_norm(x, g):
    xf = x.astype(jnp.float32)
    y = xf * lax.rsqrt(jnp.mean(xf * xf, axis=-1, keepdims=True) + EPS)
    return (y * g.astype(jnp.float32)).astype(x.dtype)


def rope(x, pos):
    half = x.shape[-1] // 2
    inv = ROPE_THETA ** (-jnp.arange(half, dtype=jnp.float32) / half)
    ang = pos.astype(jnp.float32)[:, None] * inv[None, :]
    cos = jnp.cos(ang)[:, None, :]
    sin = jnp.sin(ang)[:, None, :]
    xf = x.astype(jnp.float32)
    x1, x2 = xf[..., :half], xf[..., half:]
    return jnp.concatenate([x1 * cos - x2 * sin, x2 * cos + x1 * sin], axis=-1).astype(x.dtype)


def rg_lru(x, h0, w_a, b_a, w_x, b_x, lam):
    B, S, W = x.shape
    xf = x.astype(jnp.float32)
    xb = xf.reshape(B, S, LRU_BLOCKS, LRU_BLOCK)
    r = jax.nn.sigmoid(jnp.einsum('bsnc,ncd->bsnd', xb, w_a.astype(jnp.float32)).reshape(B, S, W) + b_a)
    i = jax.nn.sigmoid(jnp.einsum('bsnc,ncd->bsnd', xb, w_x.astype(jnp.float32)).reshape(B, S, W) + b_x)
    log_a = -LRU_C * r * jax.nn.softplus(-lam.astype(jnp.float32))
    a = jnp.exp(log_a)
    u = jnp.sqrt(-jnp.expm1(2.0 * log_a)) * (i * xf)

    def combine(left, right):
        a_l, b_l = left
        a_r, b_r = right
        return a_l * a_r, a_r * b_l + b_r

    a_cum, h = lax.associative_scan(combine, (a, u), axis=1)
    h = h + a_cum * h0.astype(jnp.float32)[:, None, :]
    return h.astype(x.dtype), h[:, -1]


def mlstm_chunk(carry, inp):
    C, n, m = carry
    q, k, v, ig, lf = inp
    L = q.shape[2]
    b = jnp.cumsum(lf, axis=-1)
    causal = jnp.tril(jnp.ones((L, L), dtype=bool))
    log_d = jnp.where(causal, b[..., :, None] - b[..., None, :] + ig[..., None, :], -jnp.inf)
    inter = b + m[..., None]
    m_t = jnp.maximum(inter, jnp.max(log_d, axis=-1))
    w = jnp.einsum('bhtd,bhsd->bhts', q, k) * jnp.exp(log_d - m_t[..., None])
    inter_w = jnp.exp(inter - m_t)
    num = jnp.einsum('bhts,bhsd->bhtd', w, v) + inter_w[..., None] * jnp.einsum('bhtd,bhde->bhte', q, C)
    den = jnp.sum(w, axis=-1) + inter_w * jnp.einsum('bhtd,bhd->bht', q, n)
    h = num / jnp.maximum(jnp.abs(den), jnp.exp(-m_t))[..., None]
    b_last = b[..., -1]
    tail = b_last[..., None] - b + ig
    m_new = jnp.maximum(b_last + m, jnp.max(tail, axis=-1))
    wk = jnp.exp(tail - m_new[..., None])
    decay = jnp.exp(b_last + m - m_new)
    C_new = decay[..., None, None] * C + jnp.einsum('bhs,bhsd,bhse->bhde', wk, k, v)
    n_new = decay[..., None] * n + jnp.einsum('bhs,bhsd->bhd', wk, k)
    return (C_new, n_new, m_new), h


def mlstm_seq(q, k, v, ig, lf, C0, n0, m0):
    B, S, H, d = q.shape
    L = min(CHUNK, S)
    nc = S // L

    def blocks(t):
        t = t.astype(jnp.float32).reshape((B, nc, L, H) + t.shape[3:])
        return jnp.moveaxis(jnp.moveaxis(t, 1, 0), 2, 3)

    carry0 = (C0.astype(jnp.float32), n0.astype(jnp.float32), m0.astype(jnp.float32))
    carry, h = lax.scan(mlstm_chunk, carry0, (blocks(q), blocks(k), blocks(v), blocks(ig), blocks(lf)))
    h = jnp.moveaxis(jnp.moveaxis(h, 3, 2), 0, 1).reshape(B, S, H, d)
    return h, carry


def banded_sink_attention(q, k_all, v_all, P, buf_valid, sinks):
    B, S = q.shape[0], q.shape[1]
    Lq = min(CHUNK, S)
    nq = S // Lq
    idx = (jnp.arange(nq) * Lq)[:, None] + jnp.arange(P + Lq)[None, :]
    kb = k_all[:, idx].astype(jnp.float32)
    vb = v_all[:, idx].astype(jnp.float32)
    valid = jnp.logical_or(idx >= P, buf_valid)
    qb = q.reshape(B, nq, Lq, A_KV_HEADS, A_GROUPS, A_HEAD_DIM).astype(jnp.float32)
    s = jnp.einsum('bnqhgd,bnkhd->bnhgqk', qb, kb) * (A_HEAD_DIM ** -0.5)
    s = jnp.where(valid[None, :, None, None, None, :], s, -jnp.inf)
    sink = sinks.astype(jnp.float32).reshape(1, 1, A_KV_HEADS, A_GROUPS, 1, 1)
    mx = jnp.maximum(jnp.max(s, axis=-1, keepdims=True), sink)
    p = jnp.exp(s - mx)
    p = p / (jnp.sum(p, axis=-1, keepdims=True) + jnp.exp(sink - mx))
    o = jnp.einsum('bnhgqk,bnkhd->bnqhgd', p, vb)
    return o.reshape(B, S, A_WIDTH).astype(q.dtype)


def hybrid_layer(x, pos, conv_buf, h0, C0, n0, m0, k_buf, v_buf, buf_valid,
                 norm1_g, w_in, conv_w, conv_b, lru_wa, lru_ba, lru_wx, lru_bx, lru_lam,
                 m_bi, m_bf, m_norm_g, qn_g, kn_g, sinks, w_oa, w_ob, w_oc, b_gate, w_out,
                 norm2_g, w_up, w_down):
    B, S, _ = x.shape
    u = rms_norm(x, norm1_g)
    z = u @ w_in
    (xa, ga, mq, mk, mv, mo, mi, mf, aq, ak, av, g_a, g_b, g_c) = jnp.split(z, IN_SPLITS, axis=-1)

    xpad = jnp.concatenate([conv_buf.astype(xa.dtype), xa], axis=1)
    xc = conv_b + xpad[:, 0:S] * conv_w[0]
    for j in range(1, CONV_W):
        xc = xc + xpad[:, j:j + S] * conv_w[j]
    conv_new = xpad[:, -(CONV_W - 1):]
    h_a, h_last = rg_lru(xc, h0, lru_wa, lru_ba, lru_wx, lru_bx, lru_lam)
    y_a = (h_a * jax.nn.gelu(ga)) @ w_oa

    q_m = mq.reshape(B, S, M_HEADS, M_HEAD_DIM)
    k_m = mk.reshape(B, S, M_HEADS, M_HEAD_DIM) * (M_HEAD_DIM ** -0.5)
    v_m = mv.reshape(B, S, M_HEADS, M_HEAD_DIM)
    ig = (mi + m_bi).astype(jnp.float32)
    lf = jax.nn.log_sigmoid((mf + m_bf).astype(jnp.float32))
    h_m, (C_new, n_new, m_new) = mlstm_seq(q_m, k_m, v_m, ig, lf, C0, n0, m0)
    h_m = rms_norm(h_m, m_norm_g.reshape(M_HEADS, M_HEAD_DIM)).reshape(B, S, M_WIDTH).astype(x.dtype)
    y_b = (h_m * jax.nn.sigmoid(mo)) @ w_ob

    q_c = rope(rms_norm(aq.reshape(B, S, A_HEADS, A_HEAD_DIM), qn_g), pos)
    k_c = rope(rms_norm(ak.reshape(B, S, A_KV_HEADS, A_HEAD_DIM), kn_g), pos)
    v_c = av.reshape(B, S, A_KV_HEADS, A_HEAD_DIM)
    P = k_buf.shape[1]
    k_all = jnp.concatenate([k_buf.astype(k_c.dtype), k_c], axis=1)
    v_all = jnp.concatenate([v_buf.astype(v_c.dtype), v_c], axis=1)
    y_c = banded_sink_attention(q_c, k_all, v_all, P, buf_valid, sinks) @ w_oc
    k_new = k_all[:, -P:]
    v_new = v_all[:, -P:]

    mix = (jax.nn.sigmoid(g_a + b_gate[0]) * y_a
           + jax.nn.sigmoid(g_b + b_gate[1]) * y_b
           + jax.nn.sigmoid(g_c + b_gate[2]) * y_c)
    x = x + mix @ w_out
    x = x + jnp.square(jax.nn.relu(rms_norm(x, norm2_g) @ w_up)) @ w_down
    return x, (conv_new, h_last, C_new, n_new, m_new, k_new, v_new)


def setup_inputs(seed: int = 0) -> dict:
    key = jax.random.key(seed)
    k = jax.random.split(key, 32)
    f32 = jnp.float32

    def nrm(kk, shape, scale):
        return jax.random.normal(kk, shape, f32) * scale

    win = min(WINDOW, PAST_LEN)
    u = jax.random.uniform(k[16], (DEPTH, LRU_WIDTH), f32, 0.9, 0.999)
    s = u ** (1.0 / LRU_C)
    return {
        'x_prompt': nrm(k[0], (BATCH, SEQ, D_MODEL), 1.0),
        'x_sample': nrm(k[1], (DEC_BATCH, DEC_SEQ, D_MODEL), 1.0),
        'state_conv': nrm(k[2], (DEPTH, DEC_BATCH, CONV_W - 1, LRU_WIDTH), 1.0),
        'state_lru': nrm(k[3], (DEPTH, DEC_BATCH, LRU_WIDTH), 0.5),
        'state_mlstm_C': nrm(k[4], (DEPTH, DEC_BATCH, M_HEADS, M_HEAD_DIM, M_HEAD_DIM), 0.05),
        'state_mlstm_n': nrm(k[5], (DEPTH, DEC_BATCH, M_HEADS, M_HEAD_DIM), 0.1),
        'state_mlstm_m': nrm(k[6], (DEPTH, DEC_BATCH, M_HEADS), 1.0),
        'cache_k': nrm(k[7], (DEPTH, DEC_BATCH, win, A_KV_HEADS, A_HEAD_DIM), 1.0),
        'cache_v': nrm(k[8], (DEPTH, DEC_BATCH, win, A_KV_HEADS, A_HEAD_DIM), 1.0),
        'norm1_g': 1.0 + nrm(k[9], (DEPTH, D_MODEL), 0.02),
        'w_in': nrm(k[10], (DEPTH, D_MODEL, IN_COLS), D_MODEL ** -0.5),
        'conv_w': nrm(k[11], (DEPTH, CONV_W, LRU_WIDTH), CONV_W ** -0.5),
        'conv_b': nrm(k[12], (DEPTH, LRU_WIDTH), 0.02),
        'lru_wa': nrm(k[13], (DEPTH, LRU_BLOCKS, LRU_BLOCK, LRU_BLOCK), LRU_BLOCK ** -0.5),
        'lru_ba': nrm(k[14], (DEPTH, LRU_WIDTH), 0.02),
        'lru_wx': nrm(k[15], (DEPTH, LRU_BLOCKS, LRU_BLOCK, LRU_BLOCK), LRU_BLOCK ** -0.5),
        'lru_bx': nrm(k[17], (DEPTH, LRU_WIDTH), 0.02),
        'lru_lam': jnp.log(s) - jnp.log1p(-s),
        'm_bi': nrm(k[18], (DEPTH, M_HEADS), 0.1),
        'm_bf': jnp.linspace(3.0, 6.0, M_HEADS, dtype=f32)[None, :] + nrm(k[19], (DEPTH, M_HEADS), 0.1),
        'm_norm_g': 1.0 + nrm(k[20], (DEPTH, M_WIDTH), 0.02),
        'qn_g': 1.0 + nrm(k[21], (DEPTH, A_HEAD_DIM), 0.02),
        'kn_g': 1.0 + nrm(k[22], (DEPTH, A_HEAD_DIM), 0.02),
        'sinks': nrm(k[23], (DEPTH, A_HEADS), 0.5),
        'w_oa': nrm(k[24], (DEPTH, LRU_WIDTH, D_MODEL), LRU_WIDTH ** -0.5),
        'w_ob': nrm(k[25], (DEPTH, M_WIDTH, D_MODEL), M_WIDTH ** -0.5),
        'w_oc': nrm(k[26], (DEPTH, A_WIDTH, D_MODEL), A_WIDTH ** -0.5),
        'b_gate': nrm(k[27], (DEPTH, 3, D_MODEL), 0.02),
        'w_out': nrm(k[28], (DEPTH, D_MODEL, D_MODEL), D_MODEL ** -0.5),
        'norm2_g': 1.0 + nrm(k[29], (DEPTH, D_MODEL), 0.02),
        'w_up': nrm(k[30], (DEPTH, D_MODEL, D_FF), D_MODEL ** -0.5),
        'w_down': nrm(k[31], (DEPTH, D_FF, D_MODEL), D_FF ** -0.5),
    }


def reference(x_prompt, x_sample, state_conv, state_lru, state_mlstm_C, state_mlstm_n, state_mlstm_m,
              cache_k, cache_v, norm1_g, w_in, conv_w, conv_b, lru_wa, lru_ba, lru_wx, lru_bx, lru_lam,
              m_bi, m_bf, m_norm_g, qn_g, kn_g, sinks, w_oa, w_ob, w_oc, b_gate, w_out, norm2_g, w_up, w_down):
    B, S, _ = x_prompt.shape
    Sd = x_sample.shape[1]
    pos_p = jnp.arange(S, dtype=jnp.int32)
    pos_s = PAST_LEN + jnp.arange(Sd, dtype=jnp.int32)
    zc = jnp.zeros((B, CONV_W - 1, LRU_WIDTH), x_prompt.dtype)
    zh = jnp.zeros((B, LRU_WIDTH), jnp.float32)
    zC = jnp.zeros((B, M_HEADS, M_HEAD_DIM, M_HEAD_DIM), jnp.float32)
    zn = jnp.zeros((B, M_HEADS, M_HEAD_DIM), jnp.float32)
    zm = jnp.zeros((B, M_HEADS), jnp.float32)
    zkv = jnp.zeros((B, WINDOW, A_KV_HEADS, A_HEAD_DIM), x_prompt.dtype)

    y_p, y_s = x_prompt, x_sample
    new_p, new_s = [], []
    for l in range(DEPTH):
        w = (norm1_g[l], w_in[l], conv_w[l], conv_b[l], lru_wa[l], lru_ba[l], lru_wx[l], lru_bx[l],
             lru_lam[l], m_bi[l], m_bf[l], m_norm_g[l], qn_g[l], kn_g[l], sinks[l], w_oa[l], w_ob[l],
             w_oc[l], b_gate[l], w_out[l], norm2_g[l], w_up[l], w_down[l])
        y_p, sp = hybrid_layer(y_p, pos_p, zc, zh, zC, zn, zm, zkv, zkv, False, *w)
        y_s, ss = hybrid_layer(y_s, pos_s, state_conv[l], state_lru[l], state_mlstm_C[l],
                               state_mlstm_n[l], state_mlstm_m[l], cache_k[l], cache_v[l], True, *w)
        new_p.append(sp)
        new_s.append(ss)
    conv_p, lru_p, C_p, n_p, m_p, k_p, v_p = [jnp.stack(t) for t in zip(*new_p)]
    conv_s, lru_s, C_s, n_s, m_s, k_s, v_s = [jnp.stack(t) for t in zip(*new_s)]
    return (y_p, y_s, conv_p, lru_p, C_p, n_p, m_p, k_p, v_p, conv_s, lru_s, C_s, n_s, m_s, k_s, v_s)
```

```python
import functools

import jax
import jax.numpy as jnp
from jax import lax
from jax.experimental import pallas as pl
from jax.experimental.pallas import tpu as pltpu

F32 = jnp.float32
BF16 = jnp.bfloat16

EPS = 1e-6
CONV_W = 4
LRU_BLOCKS = 8
LRU_C = 8.0
M_HEADS = 4
A_HEAD_DIM = 64
A_KV_HEADS = 2
ATT_CHUNK = 64
WINDOW = 128
PAST_LEN = 2048
ROPE_THETA = 10000.0

MLSTM_CHUNK = 128
PROMPT_TILE = 256
SAMPLE_TILE = 128
MLP_TILE = 256
VMEM_LIMIT = 56 * 1024 * 1024


def _mm(a, b):
    return jnp.dot(a.astype(BF16), b.astype(BF16), preferred_element_type=F32)


def _mm_nt(a, b):
    return lax.dot_general(a.astype(BF16), b.astype(BF16), (((1,), (1,)), ((), ())),
                           preferred_element_type=F32)


def _mm_tn(a, b):
    return lax.dot_general(a.astype(BF16), b.astype(BF16), (((0,), (0,)), ((), ())),
                           preferred_element_type=F32)


def _mm_f32(a, b):
    return jnp.dot(a, b, precision=lax.Precision.HIGHEST, preferred_element_type=F32)


def _mm_nt_f32(a, b):
    return lax.dot_general(a, b, (((1,), (1,)), ((), ())), precision=lax.Precision.HIGHEST,
                           preferred_element_type=F32)


def _rms(x, g):
    ms = jnp.mean(x * x, axis=-1, keepdims=True)
    return (x * lax.rsqrt(ms + EPS)) * g


def _full(shape):
    return pl.BlockSpec(shape, lambda *_: (0,) * len(shape))


def _linear_scan(a, b):
    rows = a.shape[0]
    row = lax.broadcasted_iota(jnp.int32, a.shape, 0)
    d = 1
    while d < rows:
        a_s = pltpu.roll(a, d, axis=0)
        b_s = pltpu.roll(b, d, axis=0)
        m = row >= d
        b = jnp.where(m, a * b_s + b, b)
        a = jnp.where(m, a * a_s, a)
        d *= 2
    return a, b


def _lru_kernel(x_ref, g1_ref, wxa_ref, wga_ref, wgg_ref, convw_ref, convb_ref, wa_ref, ba_ref, wx_ref,
                bx_ref, lam_ref, bg_ref, woa_ref, conv0_ref, h0_ref,
                y_ref, convn_ref, hlast_ref,
                xpad_scr, h_scr, *, tile, valid):
    t = pl.program_id(1)
    pad = 8

    @pl.when(t == 0)
    def _():
        xpad_scr[pad - (CONV_W - 1):pad, :] = conv0_ref[0]
        h_scr[...] = h0_ref[0]

    u = _rms(x_ref[0], g1_ref[...]).astype(BF16)
    xa = _mm(u, wxa_ref[...])
    ga = _mm(u, wga_ref[...])
    gg = _mm(u, wgg_ref[...])

    xpad_scr[pad:pad + tile, :] = xa
    base = pad - (CONV_W - 1)
    xc = convb_ref[...] + xpad_scr[base:base + tile, :] * convw_ref[0:1, :]
    for j in range(1, CONV_W):
        xc = xc + xpad_scr[base + j:base + j + tile, :] * convw_ref[j:j + 1, :]

    blk = xc.shape[1] // LRU_BLOCKS
    xcb = xc.astype(BF16)
    r = jnp.concatenate([_mm(xcb[:, n * blk:(n + 1) * blk], wa_ref[n]) for n in range(LRU_BLOCKS)], axis=1)
    i = jnp.concatenate([_mm(xcb[:, n * blk:(n + 1) * blk], wx_ref[n]) for n in range(LRU_BLOCKS)], axis=1)
    r = jax.nn.sigmoid(r + ba_ref[...])
    i = jax.nn.sigmoid(i + bx_ref[...])
    log_a = -LRU_C * r * jax.nn.softplus(-lam_ref[...])
    a = jnp.exp(log_a)
    th = jnp.tanh(log_a)
    uu = jnp.sqrt((-2.0 * th) / (1.0 - th)) * (i * xc)

    a_cum, h = _linear_scan(a, uu)
    h = h + a_cum * h_scr[...]

    y = _mm(h * jax.nn.gelu(ga), woa_ref[...])
    y_ref[0] = (jax.nn.sigmoid(gg + bg_ref[...]) * y).astype(y_ref.dtype)

    h_scr[...] = h[valid - 1:valid, :]
    tail = xpad_scr[pad + valid - (CONV_W - 1):pad + valid, :]
    xpad_scr[pad - (CONV_W - 1):pad, :] = tail
    convn_ref[0] = tail
    hlast_ref[0] = h[valid - 1:valid, :]


def _lru_call(x, g1, wxa, wga, wgg, convw, convb, wa, ba, wx, bx, lam, bg, woa, conv0, h0, *, tile, valid):
    B, S, D = x.shape
    W = wxa.shape[1]
    nt = S // tile
    assert nt * tile == S and (valid == tile or nt == 1)
    blk = W // LRU_BLOCKS
    kern = functools.partial(_lru_kernel, tile=tile, valid=valid)
    return pl.pallas_call(
        kern,
        grid=(B, nt),
        in_specs=[
            pl.BlockSpec((1, tile, D), lambda b, t: (b, t, 0)),
            _full((1, D)), _full((D, W)), _full((D, W)), _full((D, D)),
            _full((CONV_W, W)), _full((1, W)),
            _full((LRU_BLOCKS, blk, blk)), _full((1, W)), _full((LRU_BLOCKS, blk, blk)), _full((1, W)),
            _full((1, W)), _full((1, D)), _full((W, D)),
            pl.BlockSpec((1, CONV_W - 1, W), lambda b, t: (b, 0, 0)),
            pl.BlockSpec((1, 1, W), lambda b, t: (b, 0, 0)),
        ],
        out_specs=[
            pl.BlockSpec((1, tile, D), lambda b, t: (b, t, 0)),
            pl.BlockSpec((1, CONV_W - 1, W), lambda b, t: (b, 0, 0)),
            pl.BlockSpec((1, 1, W), lambda b, t: (b, 0, 0)),
        ],
        out_shape=[
            jax.ShapeDtypeStruct((B, S, D), BF16),
            jax.ShapeDtypeStruct((B, CONV_W - 1, W), F32),
            jax.ShapeDtypeStruct((B, 1, W), F32),
        ],
        scratch_shapes=[pltpu.VMEM((tile + 8, W), F32), pltpu.VMEM((1, W), F32)],
        compiler_params=pltpu.CompilerParams(dimension_semantics=("arbitrary", "arbitrary"),
                                             vmem_limit_bytes=VMEM_LIMIT),
        name="lru",
    )(x, g1, wxa, wga, wgg, convw, convb, wa, ba, wx, bx, lam, bg, woa, conv0, h0)


def _mlstm_kernel(x_ref, g1_ref, wq_ref, wk_ref, wv_ref, wo_ref, wif_ref, bif_ref, mg_ref, wgg_ref, bg_ref,
                  wob_ref, c0_ref, n0_ref, m0_ref,
                  y_ref, cn_ref, nn_ref, mn_ref,
                  c_scr, n_scr, m_scr, q_scr, k_scr, v_scr, hg_scr, *, tile, chunk, valid):
    t = pl.program_id(1)
    nt = pl.num_programs(1)
    hd = q_scr.shape[1] // M_HEADS

    @pl.when(t == 0)
    def _():
        c_scr[...] = c0_ref[0]
        n_scr[...] = n0_ref[0]
        m_scr[...] = m0_ref[0]

    u = _rms(x_ref[0], g1_ref[...]).astype(BF16)
    q_scr[...] = _mm(u, wq_ref[...])
    k_scr[...] = _mm(u, wk_ref[...]) * (hd ** -0.5)
    v_scr[...] = _mm(u, wv_ref[...])
    og = jax.nn.sigmoid(_mm(u, wo_ref[...]))
    gg = _mm(u, wgg_ref[...])

    gates = _mm(u, wif_ref[...]) + bif_ref[...]
    glane = lax.broadcasted_iota(jnp.int32, gates.shape, 1)
    gates = jnp.where((glane >= M_HEADS) & (glane < 2 * M_HEADS), jax.nn.log_sigmoid(gates), gates)

    rr = lax.broadcasted_iota(jnp.int32, (chunk, chunk), 0)
    cc = lax.broadcasted_iota(jnp.int32, (chunk, chunk), 1)
    causal = cc <= rr
    tril = causal.astype(F32)
    pick = (lax.broadcasted_iota(jnp.int32, (8, 128), 0) == lax.broadcasted_iota(jnp.int32, (8, 128), 1)).astype(F32)
    mlane = lax.broadcasted_iota(jnp.int32, (1, 128), 1)
    ccol = lax.broadcasted_iota(jnp.int32, (1, chunk), 1)
    crow = lax.broadcasted_iota(jnp.int32, (chunk, 1), 0)

    for c in range(tile // chunk):
        r0 = c * chunk
        vl = min(max(valid - r0, 0), chunk)
        if vl == 0:
            hg_scr[r0:r0 + chunk, :] = jnp.zeros((chunk, hg_scr.shape[1]), F32)
            continue
        g_c = gates[r0:r0 + chunk, :]
        b_all = _mm_f32(tril, g_c)
        g_rows = _mm_nt_f32(pick, g_c)
        b_rows = _mm_nt_f32(pick, b_all)
        m_row = m_scr[...]
        for h in range(M_HEADS):
            cs = slice(h * hd, (h + 1) * hd)
            q_h = q_scr[r0:r0 + chunk, cs]
            k_h = k_scr[r0:r0 + chunk, cs]
            v_h = v_scr[r0:r0 + chunk, cs]
            b_col = b_all[:, M_HEADS + h:M_HEADS + h + 1]
            ig_col = g_c[:, h:h + 1]
            b_row = b_rows[M_HEADS + h:M_HEADS + h + 1, :]
            ig_row = g_rows[h:h + 1, :]
            m_prev = m_row[:, h:h + 1]
            c_h = c_scr[h]
            n_h = n_scr[h:h + 1, :]

            log_d = jnp.where(causal, (b_col - b_row) + ig_row, -jnp.inf)
            inter = b_col + m_prev
            m_t = jnp.maximum(inter, jnp.max(log_d, axis=1, keepdims=True))
            w = _mm_nt(q_h, k_h) * jnp.exp(log_d - m_t)
            inter_w = jnp.exp(inter - m_t)
            num = _mm(w, v_h) + inter_w * _mm(q_h, c_h)
            den = jnp.sum(w, axis=1, keepdims=True) + inter_w * jnp.sum(q_h * n_h, axis=1, keepdims=True)
            hh = num / jnp.maximum(jnp.abs(den), jnp.exp(-m_t))

            b_last = b_col[vl - 1:vl, :]
            tail_row = (b_last - b_row) + ig_row
            tail_col = (b_last - b_col) + ig_col
            if vl < chunk:
                tail_row = jnp.where(ccol < vl, tail_row, -jnp.inf)
                tail_col = jnp.where(crow < vl, tail_col, -jnp.inf)
            m_new = jnp.maximum(b_last + m_prev, jnp.max(tail_row, axis=1, keepdims=True))
            decay = jnp.exp((b_last + m_prev) - m_new)
            kw = k_h * jnp.exp(tail_col - m_new)
            c_scr[h] = decay * c_h + _mm_tn(kw, v_h)
            n_scr[h:h + 1, :] = decay * n_h + jnp.sum(kw, axis=0, keepdims=True)
            m_row = jnp.where(mlane == h, m_new, m_row)

            hn = _rms(hh, mg_ref[:, cs])
            hg_scr[r0:r0 + chunk, cs] = hn * og[r0:r0 + chunk, cs]
        m_scr[...] = m_row

    y = _mm(hg_scr[...], wob_ref[...])
    y_ref[0] = (jax.nn.sigmoid(gg + bg_ref[...]) * y).astype(y_ref.dtype)

    @pl.when(t == nt - 1)
    def _():
        cn_ref[0] = c_scr[...]
        nn_ref[0] = n_scr[...]
        mn_ref[0] = m_scr[...]


def _mlstm_call(x, g1, wq, wk, wv, wo, wif, bif, mg, wgg, bg, wob, c0, n0, m0, *, tile, valid):
    B, S, D = x.shape
    MW = wq.shape[1]
    hd = MW // M_HEADS
    nt = S // tile
    chunk = MLSTM_CHUNK
    assert nt * tile == S and tile % chunk == 0 and (valid == tile or nt == 1)
    kern = functools.partial(_mlstm_kernel, tile=tile, chunk=chunk, valid=valid)
    return pl.pallas_call(
        kern,
        grid=(B, nt),
        in_specs=[
            pl.BlockSpec((1, tile, D), lambda b, t: (b, t, 0)),
            _full((1, D)), _full((D, MW)), _full((D, MW)), _full((D, MW)), _full((D, MW)),
            _full((D, 128)), _full((1, 128)), _full((1, MW)), _full((D, D)), _full((1, D)), _full((MW, D)),
            pl.BlockSpec((1, M_HEADS, hd, hd), lambda b, t: (b, 0, 0, 0)),
            pl.BlockSpec((1, M_HEADS, hd), lambda b, t: (b, 0, 0)),
            pl.BlockSpec((1, 1, 128), lambda b, t: (b, 0, 0)),
        ],
        out_specs=[
            pl.BlockSpec((1, tile, D), lambda b, t: (b, t, 0)),
            pl.BlockSpec((1, M_HEADS, hd, hd), lambda b, t: (b, 0, 0, 0)),
            pl.BlockSpec((1, M_HEADS, hd), lambda b, t: (b, 0, 0)),
            pl.BlockSpec((1, 1, 128), lambda b, t: (b, 0, 0)),
        ],
        out_shape=[
            jax.ShapeDtypeStruct((B, S, D), BF16),
            jax.ShapeDtypeStruct((B, M_HEADS, hd, hd), F32),
            jax.ShapeDtypeStruct((B, M_HEADS, hd), F32),
            jax.ShapeDtypeStruct((B, 1, 128), F32),
        ],
        scratch_shapes=[
            pltpu.VMEM((M_HEADS, hd, hd), F32), pltpu.VMEM((M_HEADS, hd), F32), pltpu.VMEM((1, 128), F32),
            pltpu.VMEM((tile, MW), F32), pltpu.VMEM((tile, MW), F32), pltpu.VMEM((tile, MW), F32),
            pltpu.VMEM((tile, MW), F32),
        ],
        compiler_params=pltpu.CompilerParams(dimension_semantics=("arbitrary", "arbitrary"),
                                             vmem_limit_bytes=VMEM_LIMIT),
        name="mlstm",
    )(x, g1, wq, wk, wv, wo, wif, bif, mg, wgg, bg, wob, c0, n0, m0)


def _seg_mean_sq(x, seg_ones):
    sq = x * x
    hi = sq.astype(BF16)
    lo = (sq - hi.astype(F32)).astype(BF16)
    tot = jnp.dot(hi, seg_ones, preferred_element_type=F32) + jnp.dot(lo, seg_ones, preferred_element_type=F32)
    return tot * (1.0 / A_HEAD_DIM)


def _rope(x, cos, sin_signed):
    width = x.shape[1]
    half = A_HEAD_DIM // 2
    lane = lax.broadcasted_iota(jnp.int32, x.shape, 1)
    partner = jnp.where((lane % A_HEAD_DIM) < half,
                        pltpu.roll(x, width - half, axis=1),
                        pltpu.roll(x, half, axis=1))
    reps = width // cos.shape[1]
    if reps > 1:
        cos = jnp.concatenate([cos] * reps, axis=1)
        sin_signed = jnp.concatenate([sin_signed] * reps, axis=1)
    return x * cos + partner * sin_signed


def _attn_kernel(sinks_ref, x_ref, g1_ref, wq_ref, wk_ref, wv_ref, wgg_ref, qg_ref, kg_ref, segq_ref, segk_ref,
                 cos_ref, sin_ref, bg_ref, woc_ref, k0_ref, v0_ref,
                 y_ref, kn_ref, vn_ref,
                 k_scr, v_scr, o_scr, *, tile, valid, has_history):
    t = pl.program_id(1)
    P = WINDOW
    qc = ATT_CHUNK
    kvw = k_scr.shape[1]
    groups_pairs = (wq_ref.shape[1] // kvw) // A_KV_HEADS

    @pl.when(t == 0)
    def _():
        k_scr[0:P, :] = k0_ref[0]
        v_scr[0:P, :] = v0_ref[0]

    u = _rms(x_ref[0], g1_ref[...]).astype(BF16)
    aq = _mm(u, wq_ref[...])
    ak = _mm(u, wk_ref[...])
    av = _mm(u, wv_ref[...])
    gg = _mm(u, wgg_ref[...])

    cos = cos_ref[...]
    sin = sin_ref[...]
    qn = aq * lax.rsqrt(_seg_mean_sq(aq, segq_ref[...]) + EPS) * qg_ref[...]
    kn = ak * lax.rsqrt(_seg_mean_sq(ak, segk_ref[...]) + EPS) * kg_ref[...]
    qr = (_rope(qn, cos, sin) * (A_HEAD_DIM ** -0.5)).astype(BF16)
    k_scr[P:P + tile, :] = _rope(kn, cos, sin)
    v_scr[P:P + tile, :] = av

    lane_kv = lax.broadcasted_iota(jnp.int32, (P + qc, kvw), 1)
    lo_half = lane_kv < A_HEAD_DIM
    key_lane = lax.broadcasted_iota(jnp.int32, (1, 2 * P), 1)
    row4 = lax.broadcasted_iota(jnp.int32, (groups_pairs * qc, 1), 0)
    out_lane = lax.broadcasted_iota(jnp.int32, (1, kvw), 1)
    zpad = jnp.zeros((2 * P - (P + qc), kvw), BF16)

    for c in range(tile // qc):
        r0 = c * qc
        own = min(max(valid - r0, 0), qc)
        if own == 0:
            o_scr[r0:r0 + qc, :] = jnp.zeros((qc, o_scr.shape[1]), F32)
            continue
        if has_history:
            first_valid = 0
        else:
            first_valid = P - jnp.minimum(t * tile + r0, P)
        key_ok = (key_lane >= first_valid) & (key_lane < P + own)

        k_win = k_scr[r0:r0 + P + qc, :]
        v_win = v_scr[r0:r0 + P + qc, :]
        k_rot = pltpu.roll(k_win, A_HEAD_DIM, axis=1)
        v_rot = pltpu.roll(v_win, A_HEAD_DIM, axis=1)
        for kvh in range(A_KV_HEADS):
            k_a, k_b = (k_win, k_rot) if kvh == 0 else (k_rot, k_win)
            v_a, v_b = (v_win, v_rot) if kvh == 0 else (v_rot, v_win)
            k_bd = jnp.concatenate([jnp.where(lo_half, k_a, 0.0).astype(BF16), zpad,
                                    jnp.where(lo_half, 0.0, k_b).astype(BF16), zpad], axis=0)
            v_bd = jnp.concatenate([jnp.where(lo_half, v_a, 0.0).astype(BF16), zpad,
                                    jnp.where(lo_half, 0.0, v_b).astype(BF16), zpad], axis=0)
            pair0 = kvh * groups_pairs
            q_st = jnp.concatenate([qr[r0:r0 + qc, (pair0 + j) * kvw:(pair0 + j + 1) * kvw]
                                    for j in range(groups_pairs)], axis=0)
            s = _mm_nt(q_st, k_bd)
            probs, inv = [], []
            for half in range(2):
                s_h = jnp.where(key_ok, s[:, half * 2 * P:(half + 1) * 2 * P], -jnp.inf)
                sink = jnp.zeros((groups_pairs * qc, 1), F32)
                for j in range(groups_pairs):
                    sink = jnp.where((row4 >= j * qc) & (row4 < (j + 1) * qc),
                                     sinks_ref[(pair0 + j) * 2 + half], sink)
                mx = jnp.maximum(jnp.max(s_h, axis=1, keepdims=True), sink)
                p = jnp.exp(s_h - mx)
                probs.append(p.astype(BF16))
                inv.append(1.0 / (jnp.sum(p, axis=1, keepdims=True) + jnp.exp(sink - mx)))
            o = jnp.dot(jnp.concatenate(probs, axis=1), v_bd, preferred_element_type=F32)
            o = o * jnp.where(out_lane < A_HEAD_DIM, inv[0], inv[1])
            for j in range(groups_pairs):
                o_scr[r0:r0 + qc, (pair0 + j) * kvw:(pair0 + j + 1) * kvw] = o[j * qc:(j + 1) * qc, :]

    y = _mm(o_scr[...], woc_ref[...])
    y_ref[0] = (jax.nn.sigmoid(gg + bg_ref[...]) * y).astype(y_ref.dtype)

    k_new = k_scr[valid:valid + P, :]
    v_new = v_scr[valid:valid + P, :]
    k_scr[0:P, :] = k_new
    v_scr[0:P, :] = v_new
    kn_ref[0] = k_new
    vn_ref[0] = v_new


def _attn_call(sinks, x, g1, wq, wk, wv, wgg, qg, kg, segq, segk, cos, sin, bg, woc, k0, v0, *,
               tile, valid, has_history):
    B, S, D = x.shape
    AW = wq.shape[1]
    KW = wk.shape[1]
    nt = S // tile
    assert nt * tile == S and tile % ATT_CHUNK == 0 and (valid == tile or nt == 1)
    kern = functools.partial(_attn_kernel, tile=tile, valid=valid, has_history=has_history)
    return pl.pallas_call(
        kern,
        grid=(B, nt),
        in_specs=[
            pl.BlockSpec(memory_space=pltpu.SMEM),
            pl.BlockSpec((1, tile, D), lambda b, t: (b, t, 0)),
            _full((1, D)), _full((D, AW)), _full((D, KW)), _full((D, KW)), _full((D, D)),
            _full((1, AW)), _full((1, KW)), _full((AW, AW)), _full((KW, KW)),
            pl.BlockSpec((tile, KW), lambda b, t: (t, 0)),
            pl.BlockSpec((tile, KW), lambda b, t: (t, 0)),
            _full((1, D)), _full((AW, D)),
            pl.BlockSpec((1, WINDOW, KW), lambda b, t: (b, 0, 0)),
            pl.BlockSpec((1, WINDOW, KW), lambda b, t: (b, 0, 0)),
        ],
        out_specs=[
            pl.BlockSpec((1, tile, D), lambda b, t: (b, t, 0)),
            pl.BlockSpec((1, WINDOW, KW), lambda b, t: (b, 0, 0)),
            pl.BlockSpec((1, WINDOW, KW), lambda b, t: (b, 0, 0)),
        ],
        out_shape=[
            jax.ShapeDtypeStruct((B, S, D), BF16),
            jax.ShapeDtypeStruct((B, WINDOW, KW), F32),
            jax.ShapeDtypeStruct((B, WINDOW, KW), F32),
        ],
        scratch_shapes=[pltpu.VMEM((WINDOW + tile, KW), F32), pltpu.VMEM((WINDOW + tile, KW), F32),
                        pltpu.VMEM((tile, AW), F32)],
        compiler_params=pltpu.CompilerParams(dimension_semantics=("arbitrary", "arbitrary"),
                                             vmem_limit_bytes=VMEM_LIMIT),
        name="attn",
    )(sinks, x, g1, wq, wk, wv, wgg, qg, kg, segq, segk, cos, sin, bg, woc, k0, v0)


def _out_kernel(x_ref, ya_ref, yb_ref, yc_ref, wout_ref, g2_ref, wup_ref, wdown_ref, o_ref):
    mix = (ya_ref[...].astype(F32) + yb_ref[...].astype(F32)) + yc_ref[...].astype(F32)
    x1 = x_ref[...] + _mm(mix, wout_ref[...])
    up = _mm(_rms(x1, g2_ref[...]), wup_ref[...])
    o_ref[...] = x1 + _mm(jnp.square(jax.nn.relu(up)), wdown_ref[...])


def _out_call(x, ya, yb, yc, wout, g2, wup, wdown, *, tile):
    N, D = x.shape
    FF = wup.shape[1]
    assert N % tile == 0
    row = lambda i: (i, 0)
    const = lambda shape: pl.BlockSpec(shape, lambda i: (0, 0), pipeline_mode=pl.Buffered(1))
    return pl.pallas_call(
        _out_kernel,
        grid=(N // tile,),
        in_specs=[pl.BlockSpec((tile, D), row), pl.BlockSpec((tile, D), row), pl.BlockSpec((tile, D), row),
                  pl.BlockSpec((tile, D), row), const((D, D)), const((1, D)), const((D, FF)), const((FF, D))],
        out_specs=pl.BlockSpec((tile, D), row),
        out_shape=jax.ShapeDtypeStruct((N, D), F32),
        compiler_params=pltpu.CompilerParams(dimension_semantics=("arbitrary",),
                                             vmem_limit_bytes=VMEM_LIMIT),
        name="out_mlp",
    )(x, ya, yb, yc, wout, g2, wup, wdown)


def _rope_tables(pos, lanes):
    half = A_HEAD_DIM // 2
    inv = ROPE_THETA ** (-jnp.arange(half, dtype=F32) / half)
    ang = pos.astype(F32)[:, None] * inv[None, :]
    cos, sin = jnp.cos(ang), jnp.sin(ang)
    reps = lanes // A_HEAD_DIM
    return (jnp.tile(jnp.concatenate([cos, cos], axis=1), (1, reps)),
            jnp.tile(jnp.concatenate([-sin, sin], axis=1), (1, reps)))


def _seg_ones(width):
    seg = jnp.arange(width) // A_HEAD_DIM
    return (seg[:, None] == seg[None, :]).astype(BF16)


def _layer(x, st, w, cos, sin, *, tile, valid, has_history):
    (norm1_g, w_in, conv_w, conv_b, lru_wa, lru_ba, lru_wx, lru_bx, lru_lam, m_bi, m_bf, m_norm_g, qn_g, kn_g,
     sinks, w_oa, w_ob, w_oc, b_gate, w_out, norm2_g, w_up, w_down) = w
    conv0, h0, c0, n0, m0, k0, v0 = st
    B, S, D = x.shape
    W = conv_w.shape[1]
    MW = m_norm_g.shape[0]
    AW = w_oc.shape[0]
    KW = A_KV_HEADS * A_HEAD_DIM
    sizes = (W, W, MW, MW, MW, MW, M_HEADS, M_HEADS, AW, KW, KW, D, D, D)
    offs = [0]
    for s_ in sizes:
        offs.append(offs[-1] + s_)
    col = lambda i: w_in[:, offs[i]:offs[i + 1]].astype(BF16)
    row = lambda v: v.reshape(1, -1)

    ya, conv_n, h_n = _lru_call(
        x, row(norm1_g), col(0), col(1), col(11), conv_w, row(conv_b), lru_wa.astype(BF16), row(lru_ba),
        lru_wx.astype(BF16), row(lru_bx), row(lru_lam), row(b_gate[0]), w_oa.astype(BF16),
        conv0, h0.reshape(B, 1, W), tile=tile, valid=valid)

    wif = jnp.zeros((D, 128), BF16).at[:, :2 * M_HEADS].set(w_in[:, offs[6]:offs[8]].astype(BF16))
    bif = jnp.zeros((1, 128), F32).at[0, :2 * M_HEADS].set(jnp.concatenate([m_bi, m_bf]))
    m0p = jnp.zeros((B, 1, 128), F32).at[:, 0, :M_HEADS].set(m0)
    yb, c_n, n_n, m_n = _mlstm_call(
        x, row(norm1_g), col(2), col(3), col(4), col(5), wif, bif, row(m_norm_g), col(12), row(b_gate[1]),
        w_ob.astype(BF16), c0, n0, m0p, tile=tile, valid=valid)

    yc, k_n, v_n = _attn_call(
        sinks, x, row(norm1_g), col(8), col(9), col(10), col(13), row(jnp.tile(qn_g, AW // A_HEAD_DIM)),
        row(jnp.tile(kn_g, A_KV_HEADS)), _seg_ones(AW), _seg_ones(KW), cos, sin, row(b_gate[2]),
        w_oc.astype(BF16), k0.reshape(B, WINDOW, KW), v0.reshape(B, WINDOW, KW),
        tile=tile, valid=valid, has_history=has_history)

    N = B * S
    x_new = _out_call(x.reshape(N, D), ya.reshape(N, D), yb.reshape(N, D), yc.reshape(N, D),
                      w_out.astype(BF16), row(norm2_g), w_up.astype(BF16), w_down.astype(BF16),
                      tile=min(MLP_TILE, N)).reshape(B, S, D)
    new_state = (conv_n, h_n.reshape(B, W), c_n, n_n, m_n[:, 0, :M_HEADS],
                 k_n.reshape(B, WINDOW, A_KV_HEADS, A_HEAD_DIM), v_n.reshape(B, WINDOW, A_KV_HEADS, A_HEAD_DIM))
    return x_new, new_state


def kernel(x_prompt, x_sample, state_conv, state_lru, state_mlstm_C, state_mlstm_n, state_mlstm_m, cache_k, cache_v, norm1_g, w_in, conv_w, conv_b, lru_wa, lru_ba, lru_wx, lru_bx, lru_lam, m_bi, m_bf, m_norm_g, qn_g, kn_g, sinks, w_oa, w_ob, w_oc, b_gate, w_out, norm2_g, w_up, w_down):
    B, S, D = x_prompt.shape
    Bs, Sd, _ = x_sample.shape
    depth = w_in.shape[0]
    W = conv_w.shape[-1]
    hd = state_mlstm_C.shape[-1]
    KW = A_KV_HEADS * A_HEAD_DIM
    p_tile = min(PROMPT_TILE, S)
    assert S % p_tile == 0 and Sd <= SAMPLE_TILE and cache_k.shape[2] == WINDOW

    cos_p, sin_p = _rope_tables(jnp.arange(S, dtype=jnp.int32), KW)
    cos_s, sin_s = _rope_tables(PAST_LEN + jnp.arange(SAMPLE_TILE, dtype=jnp.int32), KW)

    zero_state = (jnp.zeros((B, CONV_W - 1, W), F32), jnp.zeros((B, W), F32),
                  jnp.zeros((B, M_HEADS, hd, hd), F32), jnp.zeros((B, M_HEADS, hd), F32),
                  jnp.zeros((B, M_HEADS), F32),
                  jnp.zeros((B, WINDOW, A_KV_HEADS, A_HEAD_DIM), F32),
                  jnp.zeros((B, WINDOW, A_KV_HEADS, A_HEAD_DIM), F32))

    y_p = x_prompt
    y_s = jnp.pad(x_sample, ((0, 0), (0, SAMPLE_TILE - Sd), (0, 0)))
    new_p, new_s = [], []
    for l in range(depth):
        w = (norm1_g[l], w_in[l], conv_w[l], conv_b[l], lru_wa[l], lru_ba[l], lru_wx[l], lru_bx[l], lru_lam[l],
             m_bi[l], m_bf[l], m_norm_g[l], qn_g[l], kn_g[l], sinks[l], w_oa[l], w_ob[l], w_oc[l], b_gate[l],
             w_out[l], norm2_g[l], w_up[l], w_down[l])
        y_p, sp = _layer(y_p, zero_state, w, cos_p, sin_p, tile=p_tile, valid=p_tile, has_history=False)
        st = (state_conv[l], state_lru[l], state_mlstm_C[l], state_mlstm_n[l], state_mlstm_m[l],
              cache_k[l], cache_v[l])
        y_s, ss = _layer(y_s, st, w, cos_s, sin_s, tile=SAMPLE_TILE, valid=Sd, has_history=True)
        new_p.append(sp)
        new_s.append(ss)
    outs_p = [jnp.stack(t) for t in zip(*new_p)]
    outs_s = [jnp.stack(t) for t in zip(*new_s)]
    return (y_p, y_s[:, :Sd], *outs_p, *outs_s)
```

```python
import functools

import jax
import jax.numpy as jnp
from jax import lax
from jax.experimental import pallas as pl
from jax.experimental.pallas import tpu as pltpu

F32 = jnp.float32
BF16 = jnp.bfloat16

EPS = 1e-6
CONV_W = 4
LRU_BLOCKS = 8
LRU_C = 8.0
M_HEADS = 4
A_HEAD_DIM = 64
A_KV_HEADS = 2
ATT_CHUNK = 64
WINDOW = 128
PAST_LEN = 2048
ROPE_THETA = 10000.0

SUBLANES = 8
LANES = 128
MXU_DIM = 256
MLSTM_CHUNK = 256
LRU_ROWS = 32
PROMPT_TILE = 256
SAMPLE_TILE = 128
MLP_TILE = 256
VMEM_LIMIT = 60 * 1024 * 1024


def _mm(a, b):
    return jnp.dot(a.astype(BF16), b.astype(BF16), preferred_element_type=F32)


def _mm_nt(a, b):
    return lax.dot_general(a.astype(BF16), b.astype(BF16), (((1,), (1,)), ((), ())),
                           preferred_element_type=F32)


def _mm_tn(a, b):
    return lax.dot_general(a.astype(BF16), b.astype(BF16), (((0,), (0,)), ((), ())),
                           preferred_element_type=F32)


def _mm_f32(a, b):
    return jnp.dot(a, b, precision=lax.Precision.HIGHEST, preferred_element_type=F32)


def _mm_nt_f32(a, b):
    return lax.dot_general(a, b, (((1,), (1,)), ((), ())), precision=lax.Precision.HIGHEST,
                           preferred_element_type=F32)


def _rms(x, g):
    ms = jnp.mean(x * x, axis=-1, keepdims=True)
    return (x * lax.rsqrt(ms + EPS)) * g


def _linear_scan(a, b, h0):
    rows, width = a.shape
    sub = lax.broadcasted_iota(jnp.int32, (SUBLANES, width), 0)
    out = []
    carry = h0
    for g in range(rows // SUBLANES):
        a_g = a[g * SUBLANES:(g + 1) * SUBLANES, :]
        b_g = b[g * SUBLANES:(g + 1) * SUBLANES, :]
        d = 1
        while d < SUBLANES:
            m = sub >= d
            b_g = jnp.where(m, a_g * pltpu.roll(b_g, d, axis=0) + b_g, b_g)
            a_g = jnp.where(m, a_g * pltpu.roll(a_g, d, axis=0), a_g)
            d *= 2
        h_g = b_g + a_g * carry
        out.append(h_g)
        carry = h_g[SUBLANES - 1:SUBLANES, :]
    return jnp.concatenate(out, axis=0)


def _lru_init(conv0_ref, h0_ref, xpad_scr, h_scr):
    pad = SUBLANES
    if conv0_ref is None:
        xpad_scr[0:pad, :] = jnp.zeros((pad, xpad_scr.shape[1]), F32)
        h_scr[...] = jnp.zeros(h_scr.shape, F32)
    else:
        xpad_scr[pad - (CONV_W - 1):pad, :] = conv0_ref[0]
        h_scr[...] = h0_ref[0]


def _lru_section(u, wxa_ref, wga_ref, wgg_ref, convw_ref, convb_ref, wgate_ref, ba_ref, bx_ref, lam_ref,
                 bg_ref, woa_ref, convn_ref, hlast_ref, xpad_scr, h_scr, hg_scr, *, tile, valid):
    pad = SUBLANES
    ntail = CONV_W - 1

    xpad_scr[pad:pad + tile, :] = _mm(u, wxa_ref[...])
    base = pad - ntail
    xc = convb_ref[...] + xpad_scr[base:base + tile, :] * convw_ref[0:1, :]
    for j in range(1, CONV_W):
        xc = xc + xpad_scr[base + j:base + j + tile, :] * convw_ref[j:j + 1, :]

    xcb = xc.astype(BF16)
    pairs = wgate_ref.shape[0]
    gates = [jnp.dot(xcb[:, p * MXU_DIM:(p + 1) * MXU_DIM], wgate_ref[p], preferred_element_type=F32)
             for p in range(pairs)]
    r_pre = jnp.concatenate([g[:, :MXU_DIM] for g in gates], axis=1) + ba_ref[...]
    i_pre = jnp.concatenate([g[:, MXU_DIM:] for g in gates], axis=1) + bx_ref[...]
    yield
    ga = _mm(u, wga_ref[...])
    yield

    carry = h_scr[...]
    sp = jax.nn.softplus(-lam_ref[...])
    for r0 in range(0, tile, LRU_ROWS):
        rows = slice(r0, r0 + LRU_ROWS)
        if r0 >= valid:
            hg_scr[rows, :] = jnp.zeros((LRU_ROWS, hg_scr.shape[1]), F32)
            continue
        r = jax.nn.sigmoid(r_pre[rows])
        i = jax.nn.sigmoid(i_pre[rows])
        log_a = -LRU_C * r * sp
        a = jnp.exp(log_a)
        th = jnp.tanh(log_a)
        uu = jnp.sqrt((-2.0 * th) / (1.0 - th)) * (i * xc[rows])
        h = _linear_scan(a, uu, carry)
        carry = h[LRU_ROWS - 1:LRU_ROWS, :]
        if r0 < valid <= r0 + LRU_ROWS:
            h_last = h[valid - 1 - r0:valid - r0, :]
            h_scr[...] = h_last
            hlast_ref[0] = h_last
        hg_scr[rows, :] = h * jax.nn.gelu(ga[rows])
        yield

    gg = _mm(u, wgg_ref[...])
    yield
    y = _mm(hg_scr[...], woa_ref[...])
    tail = xpad_scr[pad + valid - ntail:pad + valid, :]
    xpad_scr[pad - ntail:pad, :] = tail
    convn_ref[0] = tail
    return jax.nn.sigmoid(gg + bg_ref[...]) * y


def _mlstm_init(c0_ref, n0_ref, m0_ref, c_scr, n_scr, m_scr):
    if c0_ref is None:
        c_scr[...] = jnp.zeros(c_scr.shape, F32)
        n_scr[...] = jnp.zeros(n_scr.shape, F32)
        m_scr[...] = jnp.zeros(m_scr.shape, F32)
    else:
        c_scr[...] = c0_ref[0]
        n_scr[...] = n0_ref[0]
        m_scr[...] = m0_ref[0]


def _mlstm_section(u, wq_ref, wk_ref, wv_ref, wo_ref, wif_ref, bif_ref, mg_ref, wgg_ref, bg_ref, wob_ref,
                   c_scr, n_scr, m_scr, q_scr, k_scr, v_scr, hg_scr, *, tile, chunk, valid):
    hd = q_scr.shape[1] // M_HEADS

    q_scr[...] = _mm(u, wq_ref[...])
    yield
    k_scr[...] = _mm(u, wk_ref[...]) * (hd ** -0.5)
    yield
    v_scr[...] = _mm(u, wv_ref[...])
    yield
    og = jax.nn.sigmoid(_mm(u, wo_ref[...]))
    yield
    gg = _mm(u, wgg_ref[...])
    yield

    gates = _mm(u, wif_ref[...]) + bif_ref[...]
    glane = lax.broadcasted_iota(jnp.int32, gates.shape, 1)
    gates = jnp.where((glane >= M_HEADS) & (glane < 2 * M_HEADS), jax.nn.log_sigmoid(gates), gates)

    rr = lax.broadcasted_iota(jnp.int32, (chunk, chunk), 0)
    cc = lax.broadcasted_iota(jnp.int32, (chunk, chunk), 1)
    causal = cc <= rr
    tril = causal.astype(F32)
    pick = (lax.broadcasted_iota(jnp.int32, (SUBLANES, LANES), 0)
            == lax.broadcasted_iota(jnp.int32, (SUBLANES, LANES), 1)).astype(F32)
    mlane = lax.broadcasted_iota(jnp.int32, (1, LANES), 1)
    ccol = lax.broadcasted_iota(jnp.int32, (1, chunk), 1)
    crow = lax.broadcasted_iota(jnp.int32, (chunk, 1), 0)

    for c in range(tile // chunk):
        r0 = c * chunk
        vl = min(max(valid - r0, 0), chunk)
        if vl == 0:
            hg_scr[r0:r0 + chunk, :] = jnp.zeros((chunk, hg_scr.shape[1]), F32)
            continue
        g_c = gates[r0:r0 + chunk, :]
        b_all = _mm_f32(tril, g_c)
        g_rows = _mm_nt_f32(pick, g_c)
        b_rows = _mm_nt_f32(pick, b_all)
        m_row = m_scr[...]
        for h in range(M_HEADS):
            cs = slice(h * hd, (h + 1) * hd)
            q_h = q_scr[r0:r0 + chunk, cs]
            k_h = k_scr[r0:r0 + chunk, cs]
            v_h = v_scr[r0:r0 + chunk, cs]
            b_col = b_all[:, M_HEADS + h:M_HEADS + h + 1]
            ig_col = g_c[:, h:h + 1]
            b_row = b_rows[M_HEADS + h:M_HEADS + h + 1, :]
            ig_row = g_rows[h:h + 1, :]
            m_prev = m_row[:, h:h + 1]
            c_h = c_scr[h]
            n_h = n_scr[h:h + 1, :]

            log_d = jnp.where(causal, (b_col - b_row) + ig_row, -jnp.inf)
            inter = b_col + m_prev
            m_t = jnp.maximum(inter, jnp.max(log_d, axis=1, keepdims=True))
            w = _mm_nt(q_h, k_h) * jnp.exp(log_d - m_t)
            inter_w = jnp.exp(inter - m_t)
            num = _mm(w, v_h) + inter_w * _mm(q_h, c_h)
            den = jnp.sum(w, axis=1, keepdims=True) + inter_w * jnp.sum(q_h * n_h, axis=1, keepdims=True)
            hh = num / jnp.maximum(jnp.abs(den), jnp.exp(-m_t))

            b_last = b_col[vl - 1:vl, :]
            tail_row = (b_last - b_row) + ig_row
            tail_col = (b_last - b_col) + ig_col
            if vl < chunk:
                tail_row = jnp.where(ccol < vl, tail_row, -jnp.inf)
                tail_col = jnp.where(crow < vl, tail_col, -jnp.inf)
            m_new = jnp.maximum(b_last + m_prev, jnp.max(tail_row, axis=1, keepdims=True))
            decay = jnp.exp((b_last + m_prev) - m_new)
            kw = k_h * jnp.exp(tail_col - m_new)
            c_scr[h] = decay * c_h + _mm_tn(kw, v_h)
            n_scr[h:h + 1, :] = decay * n_h + jnp.sum(kw, axis=0, keepdims=True)
            m_row = jnp.where(mlane == h, m_new, m_row)

            hn = _rms(hh, mg_ref[:, cs])
            hg_scr[r0:r0 + chunk, cs] = hn * og[r0:r0 + chunk, cs]
            if h < M_HEADS - 1:
                yield
        m_scr[...] = m_row
        yield

    y = _mm(hg_scr[...], wob_ref[...])
    return jax.nn.sigmoid(gg + bg_ref[...]) * y


def _seg_mean_sq(x, seg_ones):
    sq = x * x
    hi = sq.astype(BF16)
    lo = (sq - hi.astype(F32)).astype(BF16)
    tot = jnp.dot(hi, seg_ones, preferred_element_type=F32) + jnp.dot(lo, seg_ones, preferred_element_type=F32)
    return tot * (1.0 / A_HEAD_DIM)


def _rope(x, cos, sin_signed):
    width = x.shape[1]
    half = A_HEAD_DIM // 2
    lane = lax.broadcasted_iota(jnp.int32, x.shape, 1)
    partner = jnp.where((lane % A_HEAD_DIM) < half,
                        pltpu.roll(x, width - half, axis=1),
                        pltpu.roll(x, half, axis=1))
    reps = width // cos.shape[1]
    if reps > 1:
        cos = jnp.concatenate([cos] * reps, axis=1)
        sin_signed = jnp.concatenate([sin_signed] * reps, axis=1)
    return x * cos + partner * sin_signed


def _attn_init(k0_ref, v0_ref, k_scr, v_scr):
    P, kvw = WINDOW, k_scr.shape[1]
    if k0_ref is None:
        k_scr[0:P, :] = jnp.zeros((P, kvw), F32)
        v_scr[0:P, :] = jnp.zeros((P, kvw), F32)
    else:
        k_scr[0:P, :] = k0_ref[0]
        v_scr[0:P, :] = v0_ref[0]


def _attn_section(u, t, sinks_ref, wq_ref, wk_ref, wv_ref, wgg_ref, qg_ref, kg_ref, segq_ref, segk_ref,
                  cos_ref, sin_ref, bg_ref, woc_ref, kn_ref, vn_ref,
                  k_scr, v_scr, o_scr, *, tile, valid, has_history):
    P = WINDOW
    qc = ATT_CHUNK
    kvw = k_scr.shape[1]
    groups_pairs = (wq_ref.shape[1] // kvw) // A_KV_HEADS

    aq = _mm(u, wq_ref[...])
    yield
    ak = _mm(u, wk_ref[...])
    av = _mm(u, wv_ref[...])
    cos = cos_ref[...]
    sin = sin_ref[...]
    kn = ak * lax.rsqrt(_seg_mean_sq(ak, segk_ref[...]) + EPS) * kg_ref[...]
    k_scr[P:P + tile, :] = _rope(kn, cos, sin)
    v_scr[P:P + tile, :] = av
    yield
    qn = aq * lax.rsqrt(_seg_mean_sq(aq, segq_ref[...]) + EPS) * qg_ref[...]
    qr = (_rope(qn, cos, sin) * (A_HEAD_DIM ** -0.5)).astype(BF16)
    yield
    gg = _mm(u, wgg_ref[...])
    yield

    lane_kv = lax.broadcasted_iota(jnp.int32, (P + qc, kvw), 1)
    lo_half = lane_kv < A_HEAD_DIM
    key_lane = lax.broadcasted_iota(jnp.int32, (1, 2 * P), 1)
    row4 = lax.broadcasted_iota(jnp.int32, (groups_pairs * qc, 1), 0)
    out_lane = lax.broadcasted_iota(jnp.int32, (1, kvw), 1)
    zpad = jnp.zeros((2 * P - (P + qc), kvw), BF16)

    for c in range(tile // qc):
        r0 = c * qc
        own = min(max(valid - r0, 0), qc)
        if own == 0:
            o_scr[r0:r0 + qc, :] = jnp.zeros((qc, o_scr.shape[1]), F32)
            continue
        if has_history:
            first_valid = 0
        else:
            first_valid = P - jnp.minimum(t * tile + r0, P)
        key_ok = (key_lane >= first_valid) & (key_lane < P + own)

        k_win = k_scr[r0:r0 + P + qc, :]
        v_win = v_scr[r0:r0 + P + qc, :]
        k_rot = pltpu.roll(k_win, A_HEAD_DIM, axis=1)
        v_rot = pltpu.roll(v_win, A_HEAD_DIM, axis=1)
        for kvh in range(A_KV_HEADS):
            k_a, k_b = (k_win, k_rot) if kvh == 0 else (k_rot, k_win)
            v_a, v_b = (v_win, v_rot) if kvh == 0 else (v_rot, v_win)
            k_bd = jnp.concatenate([jnp.where(lo_half, k_a, 0.0).astype(BF16), zpad,
                                    jnp.where(lo_half, 0.0, k_b).astype(BF16), zpad], axis=0)
            v_bd = jnp.concatenate([jnp.where(lo_half, v_a, 0.0).astype(BF16), zpad,
                                    jnp.where(lo_half, 0.0, v_b).astype(BF16), zpad], axis=0)
            pair0 = kvh * groups_pairs
            q_st = jnp.concatenate([qr[r0:r0 + qc, (pair0 + j) * kvw:(pair0 + j + 1) * kvw]
                                    for j in range(groups_pairs)], axis=0)
            s = _mm_nt(q_st, k_bd)
            probs, inv = [], []
            for half in range(2):
                s_h = jnp.where(key_ok, s[:, half * 2 * P:(half + 1) * 2 * P], -jnp.inf)
                sink = jnp.zeros((groups_pairs * qc, 1), F32)
                for j in range(groups_pairs):
                    sink = jnp.where((row4 >= j * qc) & (row4 < (j + 1) * qc),
                                     sinks_ref[(pair0 + j) * 2 + half], sink)
                mx = jnp.maximum(jnp.max(s_h, axis=1, keepdims=True), sink)
                p = jnp.exp(s_h - mx)
                probs.append(p.astype(BF16))
                inv.append(1.0 / (jnp.sum(p, axis=1, keepdims=True) + jnp.exp(sink - mx)))
            o = jnp.dot(jnp.concatenate(probs, axis=1), v_bd, preferred_element_type=F32)
            o = o * jnp.where(out_lane < A_HEAD_DIM, inv[0], inv[1])
            for j in range(groups_pairs):
                o_scr[r0:r0 + qc, (pair0 + j) * kvw:(pair0 + j + 1) * kvw] = o[j * qc:(j + 1) * qc, :]
            yield

    y = _mm(o_scr[...], woc_ref[...])

    k_new = k_scr[valid:valid + P, :]
    v_new = v_scr[valid:valid + P, :]
    k_scr[0:P, :] = k_new
    v_scr[0:P, :] = v_new
    kn_ref[0] = k_new
    vn_ref[0] = v_new
    return jax.nn.sigmoid(gg + bg_ref[...]) * y


N_LRU_W, N_MLSTM_W, N_ATTN_W = 11, 10, 12
N_STATE = 7


def _interleave(sections):
    results = [None] * len(sections)
    live = list(range(len(sections)))
    while live:
        for idx in list(live):
            try:
                next(sections[idx])
            except StopIteration as done:
                results[idx] = done.value
                live.remove(idx)
    return results


def _mixer_kernel(*refs, tile, chunk, valid, has_history):
    it = iter(refs)
    take = lambda n: [next(it) for _ in range(n)]
    sinks_ref, x_ref, g1_ref = take(3)
    lru_w, mlstm_w, attn_w = take(N_LRU_W), take(N_MLSTM_W), take(N_ATTN_W)
    (wout_ref,) = take(1)
    st_in = take(N_STATE) if has_history else [None] * N_STATE
    (y_ref,) = take(1)
    st_out = take(N_STATE)
    xpad_scr, h_scr, lh_scr, c_scr, n_scr, m_scr, q_scr, k_scr, v_scr, hg_scr, ak_scr, av_scr, o_scr = take(13)

    t = pl.program_id(1)
    nt = pl.num_programs(1)

    @pl.when(t == 0)
    def _():
        _lru_init(st_in[0], st_in[1], xpad_scr, h_scr)
        _mlstm_init(st_in[2], st_in[3], st_in[4], c_scr, n_scr, m_scr)
        _attn_init(st_in[5], st_in[6], ak_scr, av_scr)

    x = x_ref[0]
    u = _rms(x, g1_ref[...]).astype(BF16)
    y_a, y_b, y_c = _interleave([
        _lru_section(u, *lru_w, st_out[0], st_out[1], xpad_scr, h_scr, lh_scr, tile=tile, valid=valid),
        _mlstm_section(u, *mlstm_w, c_scr, n_scr, m_scr, q_scr, k_scr, v_scr, hg_scr,
                       tile=tile, chunk=chunk, valid=valid),
        _attn_section(u, t, sinks_ref, *attn_w, st_out[5], st_out[6], ak_scr, av_scr, o_scr,
                      tile=tile, valid=valid, has_history=has_history)])
    y_ref[0] = x + _mm((y_a + y_b) + y_c, wout_ref[...])

    @pl.when(t == nt - 1)
    def _():
        st_out[2][0] = c_scr[...]
        st_out[3][0] = n_scr[...]
        st_out[4][0] = m_scr[...]


def _mixer_call(sinks, x, g1, lru_w, mlstm_w, attn_w, wout, state, *, tile, valid):
    B, S, D = x.shape
    W = lru_w[0].shape[1]
    MW = mlstm_w[0].shape[1]
    AW = attn_w[0].shape[1]
    KW = attn_w[1].shape[1]
    hd = MW // M_HEADS
    nt = S // tile
    chunk = min(MLSTM_CHUNK, tile)
    has_history = state is not None
    assert nt * tile == S and tile % chunk == 0 and tile % ATT_CHUNK == 0 and (valid == tile or nt == 1)
    assert len(lru_w) == N_LRU_W and len(mlstm_w) == N_MLSTM_W and len(attn_w) == N_ATTN_W

    def const(a):
        return pl.BlockSpec(a.shape, lambda b, t, _n=a.ndim: (0,) * _n, pipeline_mode=pl.Buffered(1))

    def per_batch(shape):
        return pl.BlockSpec((1,) + shape[1:], lambda b, t, _n=len(shape): (b,) + (0,) * (_n - 1))

    cos, sin = attn_w[8], attn_w[9]
    attn_specs = [const(a) for a in attn_w]
    attn_specs[8] = pl.BlockSpec((tile, KW), lambda b, t: (t, 0))
    attn_specs[9] = pl.BlockSpec((tile, KW), lambda b, t: (t, 0))
    assert cos.shape == (S, KW) and sin.shape == (S, KW)

    state_shapes = [(B, CONV_W - 1, W), (B, 1, W), (B, M_HEADS, hd, hd), (B, M_HEADS, hd), (B, 1, LANES),
                    (B, WINDOW, KW), (B, WINDOW, KW)]
    in_specs = ([pl.BlockSpec(memory_space=pltpu.SMEM),
                 pl.BlockSpec((1, tile, D), lambda b, t: (b, t, 0)), const(g1)]
                + [const(a) for a in lru_w] + [const(a) for a in mlstm_w] + attn_specs + [const(wout)])
    args = [sinks, x, g1, *lru_w, *mlstm_w, *attn_w, wout]
    if has_history:
        in_specs += [per_batch(s) for s in state_shapes]
        args += list(state)
    kern = functools.partial(_mixer_kernel, tile=tile, chunk=chunk, valid=valid, has_history=has_history)
    return pl.pallas_call(
        kern,
        grid=(B, nt),
        in_specs=in_specs,
        out_specs=[pl.BlockSpec((1, tile, D), lambda b, t: (b, t, 0))] + [per_batch(s) for s in state_shapes],
        out_shape=[jax.ShapeDtypeStruct((B, S, D), F32)] + [jax.ShapeDtypeStruct(s, F32) for s in state_shapes],
        scratch_shapes=[
            pltpu.VMEM((tile + SUBLANES, W), F32), pltpu.VMEM((1, W), F32), pltpu.VMEM((tile, W), F32),
            pltpu.VMEM((M_HEADS, hd, hd), F32), pltpu.VMEM((M_HEADS, hd), F32), pltpu.VMEM((1, LANES), F32),
            pltpu.VMEM((tile, MW), F32), pltpu.VMEM((tile, MW), F32), pltpu.VMEM((tile, MW), F32),
            pltpu.VMEM((tile, MW), F32),
            pltpu.VMEM((WINDOW + tile, KW), F32), pltpu.VMEM((WINDOW + tile, KW), F32),
            pltpu.VMEM((tile, AW), F32),
        ],
        compiler_params=pltpu.CompilerParams(dimension_semantics=("arbitrary", "arbitrary"),
                                             vmem_limit_bytes=VMEM_LIMIT),
        name="mixer",
    )(*args)


def _mlp_kernel(x_ref, g2_ref, wup_ref, wdown_ref, o_ref):
    x = x_ref[...]
    up = _mm(_rms(x, g2_ref[...]), wup_ref[...])
    o_ref[...] = x + _mm(jnp.square(jax.nn.relu(up)), wdown_ref[...])


def _mlp_call(x, g2, wup, wdown, *, tile):
    N, D = x.shape
    FF = wup.shape[1]
    assert N % tile == 0
    row = lambda i: (i, 0)
    const = lambda shape: pl.BlockSpec(shape, lambda i: (0, 0), pipeline_mode=pl.Buffered(1))
    return pl.pallas_call(
        _mlp_kernel,
        grid=(N // tile,),
        in_specs=[pl.BlockSpec((tile, D), row), const((1, D)), const((D, FF)), const((FF, D))],
        out_specs=pl.BlockSpec((tile, D), row),
        out_shape=jax.ShapeDtypeStruct((N, D), F32),
        compiler_params=pltpu.CompilerParams(dimension_semantics=("arbitrary",),
                                             vmem_limit_bytes=VMEM_LIMIT),
        name="mlp",
    )(x, g2, wup, wdown)


def _rope_tables(pos, lanes):
    half = A_HEAD_DIM // 2
    inv = ROPE_THETA ** (-jnp.arange(half, dtype=F32) / half)
    ang = pos.astype(F32)[:, None] * inv[None, :]
    cos, sin = jnp.cos(ang), jnp.sin(ang)
    reps = lanes // A_HEAD_DIM
    return (jnp.tile(jnp.concatenate([cos, cos], axis=1), (1, reps)),
            jnp.tile(jnp.concatenate([-sin, sin], axis=1), (1, reps)))


def _seg_ones(width):
    seg = jnp.arange(width) // A_HEAD_DIM
    return (seg[:, None] == seg[None, :]).astype(BF16)


def _paired_gate_weights(wa, wx):
    nb, blk, _ = wa.shape
    per = MXU_DIM // blk
    z = jnp.zeros((blk, blk), wa.dtype)
    out = []
    for p in range(nb // per):
        rows = []
        for i in range(per):
            rows.append(jnp.concatenate([wa[p * per + i] if j == i else z for j in range(per)]
                                        + [wx[p * per + i] if j == i else z for j in range(per)], axis=1))
        out.append(jnp.concatenate(rows, axis=0))
    return jnp.stack(out).astype(BF16)


def _layer_weights(w):
    (norm1_g, w_in, conv_w, conv_b, lru_wa, lru_ba, lru_wx, lru_bx, lru_lam, m_bi, m_bf, m_norm_g, qn_g, kn_g,
     sinks, w_oa, w_ob, w_oc, b_gate, w_out, norm2_g, w_up, w_down) = w
    D = w_in.shape[0]
    W = conv_w.shape[1]
    MW = m_norm_g.shape[0]
    AW = w_oc.shape[0]
    KW = A_KV_HEADS * A_HEAD_DIM
    sizes = (W, W, MW, MW, MW, MW, M_HEADS, M_HEADS, AW, KW, KW, D, D, D)
    offs = [0]
    for s_ in sizes:
        offs.append(offs[-1] + s_)
    col = lambda i: w_in[:, offs[i]:offs[i + 1]].astype(BF16)
    row = lambda v: v.reshape(1, -1)

    lru_w = [col(0), col(1), col(11), conv_w, row(conv_b), _paired_gate_weights(lru_wa, lru_wx), row(lru_ba),
             row(lru_bx), row(lru_lam), row(b_gate[0]), w_oa.astype(BF16)]
    wif = jnp.zeros((D, LANES), BF16).at[:, :2 * M_HEADS].set(w_in[:, offs[6]:offs[8]].astype(BF16))
    bif = jnp.zeros((1, LANES), F32).at[0, :2 * M_HEADS].set(jnp.concatenate([m_bi, m_bf]))
    mlstm_w = [col(2), col(3), col(4), col(5), wif, bif, row(m_norm_g), col(12), row(b_gate[1]),
               w_ob.astype(BF16)]
    attn_w = [col(8), col(9), col(10), col(13), row(jnp.tile(qn_g, AW // A_HEAD_DIM)),
              row(jnp.tile(kn_g, A_KV_HEADS)), _seg_ones(AW), _seg_ones(KW), None, None, row(b_gate[2]),
              w_oc.astype(BF16)]
    return (sinks, row(norm1_g), lru_w, mlstm_w, attn_w, w_out.astype(BF16),
            row(norm2_g), w_up.astype(BF16), w_down.astype(BF16))


def _layer(x, state, lw, cos, sin, *, tile, valid):
    sinks, g1, lru_w, mlstm_w, attn_w, wout, g2, wup, wdown = lw
    B, S, D = x.shape
    attn_w = attn_w[:8] + [cos, sin] + attn_w[10:]
    outs = _mixer_call(sinks, x, g1, lru_w, mlstm_w, attn_w, wout, state, tile=tile, valid=valid)
    x1, new_state = outs[0], outs[1:]
    N = B * S
    x2 = _mlp_call(x1.reshape(N, D), g2, wup, wdown, tile=min(MLP_TILE, N)).reshape(B, S, D)
    return x2, new_state


def kernel(x_prompt, x_sample, state_conv, state_lru, state_mlstm_C, state_mlstm_n, state_mlstm_m, cache_k, cache_v, norm1_g, w_in, conv_w, conv_b, lru_wa, lru_ba, lru_wx, lru_bx, lru_lam, m_bi, m_bf, m_norm_g, qn_g, kn_g, sinks, w_oa, w_ob, w_oc, b_gate, w_out, norm2_g, w_up, w_down):
    B, S, D = x_prompt.shape
    Bs, Sd, _ = x_sample.shape
    depth = w_in.shape[0]
    W = conv_w.shape[-1]
    KW = A_KV_HEADS * A_HEAD_DIM
    p_tile = min(PROMPT_TILE, S)
    assert S % p_tile == 0 and Sd <= SAMPLE_TILE and cache_k.shape[2] == WINDOW

    cos_p, sin_p = _rope_tables(jnp.arange(S, dtype=jnp.int32), KW)
    cos_s, sin_s = _rope_tables(PAST_LEN + jnp.arange(SAMPLE_TILE, dtype=jnp.int32), KW)

    y_p = x_prompt
    y_s = jnp.pad(x_sample, ((0, 0), (0, SAMPLE_TILE - Sd), (0, 0)))
    new_p, new_s = [], []
    for l in range(depth):
        lw = _layer_weights((norm1_g[l], w_in[l], conv_w[l], conv_b[l], lru_wa[l], lru_ba[l], lru_wx[l], lru_bx[l],
                             lru_lam[l], m_bi[l], m_bf[l], m_norm_g[l], qn_g[l], kn_g[l], sinks[l], w_oa[l], w_ob[l],
                             w_oc[l], b_gate[l], w_out[l], norm2_g[l], w_up[l], w_down[l]))
        y_p, sp = _layer(y_p, None, lw, cos_p, sin_p, tile=p_tile, valid=p_tile)
        m0 = jnp.zeros((Bs, 1, LANES), F32).at[:, 0, :M_HEADS].set(state_mlstm_m[l])
        st = (state_conv[l], state_lru[l].reshape(Bs, 1, W), state_mlstm_C[l], state_mlstm_n[l], m0,
              cache_k[l].reshape(Bs, WINDOW, KW), cache_v[l].reshape(Bs, WINDOW, KW))
        y_s, ss = _layer(y_s, st, lw, cos_s, sin_s, tile=SAMPLE_TILE, valid=Sd)
        new_p.append(sp)
        new_s.append(ss)

    def unpack(states):
        conv, h, c, n, m, k, v = [jnp.stack(s) for s in zip(*states)]
        nb = conv.shape[1]
        return (conv, h.reshape(depth, nb, W), c, n, m[:, :, 0, :M_HEADS],
                k.reshape(depth, nb, WINDOW, A_KV_HEADS, A_HEAD_DIM),
                v.reshape(depth, nb, WINDOW, A_KV_HEADS, A_HEAD_DIM))

    return (y_p, y_s[:, :Sd], *unpack(new_p), *unpack(new_s))
```

```python
import functools

import jax
import jax.numpy as jnp
from jax import lax
from jax.experimental import pallas as pl
from jax.experimental.pallas import tpu as pltpu

F32 = jnp.float32
BF16 = jnp.bfloat16

EPS = 1e-6
CONV_W = 4
LRU_BLOCKS = 8
LRU_C = 8.0
M_HEADS = 4
A_HEAD_DIM = 64
A_KV_HEADS = 2
ATT_CHUNK = 64
WINDOW = 128
PAST_LEN = 2048
ROPE_THETA = 10000.0

SUBLANES = 8
LANES = 128
MXU_DIM = 256
MLSTM_CHUNK = 256
LRU_ROWS = 32
PROMPT_TILE = 256
SAMPLE_TILE = 128
MLP_TILE = 512
VMEM_LIMIT = 60 * 1024 * 1024


def _mm(a, b):
    return jnp.dot(a.astype(BF16), b.astype(BF16), preferred_element_type=F32)


def _mm_nt(a, b):
    return lax.dot_general(a.astype(BF16), b.astype(BF16), (((1,), (1,)), ((), ())),
                           preferred_element_type=F32)


def _mm_tn(a, b):
    return lax.dot_general(a.astype(BF16), b.astype(BF16), (((0,), (0,)), ((), ())),
                           preferred_element_type=F32)


def _mm_split3(a, b01):
    hi = a.astype(BF16)
    rest = a - hi.astype(F32)
    mid = rest.astype(BF16)
    lo = (rest - mid.astype(F32)).astype(BF16)
    dot = lambda p: jnp.dot(p, b01, preferred_element_type=F32)
    return (dot(hi) + dot(mid)) + dot(lo)


def _cumsum_lanes(x, triu01):
    return _mm_split3(x, triu01)


def _rms(x, g):
    ms = jnp.mean(x * x, axis=-1, keepdims=True)
    return (x * lax.rsqrt(ms + EPS)) * g


def _linear_scan(a, b, h0):
    rows, width = a.shape
    sub = lax.broadcasted_iota(jnp.int32, (SUBLANES, width), 0)
    out = []
    carry = h0
    for g in range(rows // SUBLANES):
        a_g = a[g * SUBLANES:(g + 1) * SUBLANES, :]
        b_g = b[g * SUBLANES:(g + 1) * SUBLANES, :]
        d = 1
        while d < SUBLANES:
            m = sub >= d
            b_g = jnp.where(m, a_g * pltpu.roll(b_g, d, axis=0) + b_g, b_g)
            a_g = jnp.where(m, a_g * pltpu.roll(a_g, d, axis=0), a_g)
            d *= 2
        h_g = b_g + a_g * carry
        out.append(h_g)
        carry = h_g[SUBLANES - 1:SUBLANES, :]
    return jnp.concatenate(out, axis=0)


def _lru_init(conv0_ref, h0_ref, xpad_scr, h_scr):
    pad = SUBLANES
    if conv0_ref is None:
        xpad_scr[0:pad, :] = jnp.zeros((pad, xpad_scr.shape[1]), F32)
        h_scr[...] = jnp.zeros(h_scr.shape, F32)
    else:
        xpad_scr[pad - (CONV_W - 1):pad, :] = conv0_ref[0]
        h_scr[...] = h0_ref[0]


def _lru_section(u, wxa_ref, wga_ref, wgg_ref, convw_ref, convb_ref, wgate_ref, ba_ref, bx_ref, lam_ref,
                 bg_ref, woa_ref, convn_ref, hlast_ref, xpad_scr, h_scr, hg_scr, *, tile, valid):
    pad = SUBLANES
    ntail = CONV_W - 1

    xpad_scr[pad:pad + tile, :] = _mm(u, wxa_ref[...])
    base = pad - ntail
    xc = convb_ref[...] + xpad_scr[base:base + tile, :] * convw_ref[0:1, :]
    for j in range(1, CONV_W):
        xc = xc + xpad_scr[base + j:base + j + tile, :] * convw_ref[j:j + 1, :]

    xcb = xc.astype(BF16)
    pairs = wgate_ref.shape[0]
    gates = [jnp.dot(xcb[:, p * MXU_DIM:(p + 1) * MXU_DIM], wgate_ref[p], preferred_element_type=F32)
             for p in range(pairs)]
    r_pre = jnp.concatenate([g[:, :MXU_DIM] for g in gates], axis=1) + ba_ref[...]
    i_pre = jnp.concatenate([g[:, MXU_DIM:] for g in gates], axis=1) + bx_ref[...]
    yield
    ga = _mm(u, wga_ref[...])
    yield

    carry = h_scr[...]
    sp = jax.nn.softplus(-lam_ref[...])
    for r0 in range(0, tile, LRU_ROWS):
        rows = slice(r0, r0 + LRU_ROWS)
        if r0 >= valid:
            hg_scr[rows, :] = jnp.zeros((LRU_ROWS, hg_scr.shape[1]), F32)
            continue
        r = jax.nn.sigmoid(r_pre[rows])
        i = jax.nn.sigmoid(i_pre[rows])
        log_a = -LRU_C * r * sp
        a = jnp.exp(log_a)
        th = jnp.tanh(log_a)
        uu = jnp.sqrt((-2.0 * th) / (1.0 - th)) * (i * xc[rows])
        h = _linear_scan(a, uu, carry)
        carry = h[LRU_ROWS - 1:LRU_ROWS, :]
        if r0 < valid <= r0 + LRU_ROWS:
            h_last = h[valid - 1 - r0:valid - r0, :]
            h_scr[...] = h_last
            hlast_ref[0] = h_last
        hg_scr[rows, :] = h * jax.nn.gelu(ga[rows])
        yield

    gg = _mm(u, wgg_ref[...])
    yield
    y = _mm(hg_scr[...], woa_ref[...])
    tail = xpad_scr[pad + valid - ntail:pad + valid, :]
    xpad_scr[pad - ntail:pad, :] = tail
    convn_ref[0] = tail
    return jax.nn.sigmoid(gg + bg_ref[...]) * y


def _mlstm_init(c0_ref, n0_ref, m0_ref, c_scr, n_scr, m_scr):
    if c0_ref is None:
        c_scr[...] = jnp.zeros(c_scr.shape, F32)
        n_scr[...] = jnp.zeros(n_scr.shape, F32)
        m_scr[...] = jnp.zeros(m_scr.shape, F32)
    else:
        c_scr[...] = c0_ref[0]
        n_scr[...] = n0_ref[0]
        m_scr[...] = m0_ref[0]


def _mlstm_section(u, wq_ref, wk_ref, wv_ref, wo_ref, wif_ref, bif_ref, mg_ref, wgg_ref, bg_ref, wob_ref,
                   c_scr, n_scr, m_scr, q_scr, k_scr, v_scr, hg_scr, *, tile, chunk, valid):
    hd = q_scr.shape[1] // M_HEADS

    q_scr[...] = _mm(u, wq_ref[...])
    yield
    k_scr[...] = _mm(u, wk_ref[...]) * (hd ** -0.5)
    yield
    v_scr[...] = _mm(u, wv_ref[...])
    yield
    og = jax.nn.sigmoid(_mm(u, wo_ref[...]))
    yield
    gg = _mm(u, wgg_ref[...])
    yield

    gates = _mm(u, wif_ref[...]) + bif_ref[...]
    glane = lax.broadcasted_iota(jnp.int32, gates.shape, 1)
    gates = jnp.where((glane >= M_HEADS) & (glane < 2 * M_HEADS), jax.nn.log_sigmoid(gates), gates)

    rr = lax.broadcasted_iota(jnp.int32, (chunk, chunk), 0)
    cc = lax.broadcasted_iota(jnp.int32, (chunk, chunk), 1)
    causal = cc <= rr
    triu = (rr <= cc).astype(BF16)
    grow = 2 * SUBLANES
    mlane = lax.broadcasted_iota(jnp.int32, (1, LANES), 1)
    ccol = lax.broadcasted_iota(jnp.int32, (1, chunk), 1)
    crow = lax.broadcasted_iota(jnp.int32, (chunk, 1), 0)

    for c in range(tile // chunk):
        r0 = c * chunk
        vl = min(max(valid - r0, 0), chunk)
        if vl == 0:
            hg_scr[r0:r0 + chunk, :] = jnp.zeros((chunk, hg_scr.shape[1]), F32)
            continue
        g_c = gates[r0:r0 + chunk, :]
        g_rows = g_c.T[0:grow, :]
        b_rows = _cumsum_lanes(g_rows, triu)
        b_all = jnp.concatenate([b_rows, jnp.zeros((LANES - grow, chunk), F32)], axis=0).T
        m_row = m_scr[...]
        for h in range(M_HEADS):
            cs = slice(h * hd, (h + 1) * hd)
            q_h = q_scr[r0:r0 + chunk, cs]
            k_h = k_scr[r0:r0 + chunk, cs]
            v_h = v_scr[r0:r0 + chunk, cs]
            b_col = b_all[:, M_HEADS + h:M_HEADS + h + 1]
            ig_col = g_c[:, h:h + 1]
            b_row = b_rows[M_HEADS + h:M_HEADS + h + 1, :]
            ig_row = g_rows[h:h + 1, :]
            m_prev = m_row[:, h:h + 1]
            c_h = c_scr[h]
            n_h = n_scr[h:h + 1, :]

            log_d = jnp.where(causal, (b_col - b_row) + ig_row, -jnp.inf)
            inter = b_col + m_prev
            m_t = jnp.maximum(inter, jnp.max(log_d, axis=1, keepdims=True))
            w = _mm_nt(q_h, k_h) * jnp.exp(log_d - m_t)
            inter_w = jnp.exp(inter - m_t)
            num = _mm(w, v_h) + inter_w * _mm(q_h, c_h)
            den = jnp.sum(w, axis=1, keepdims=True) + inter_w * jnp.sum(q_h * n_h, axis=1, keepdims=True)
            hh = num / jnp.maximum(jnp.abs(den), jnp.exp(-m_t))

            b_last = b_col[vl - 1:vl, :]
            tail_row = (b_last - b_row) + ig_row
            tail_col = (b_last - b_col) + ig_col
            if vl < chunk:
                tail_row = jnp.where(ccol < vl, tail_row, -jnp.inf)
                tail_col = jnp.where(crow < vl, tail_col, -jnp.inf)
            m_new = jnp.maximum(b_last + m_prev, jnp.max(tail_row, axis=1, keepdims=True))
            decay = jnp.exp((b_last + m_prev) - m_new)
            kw = k_h * jnp.exp(tail_col - m_new)
            c_scr[h] = decay * c_h + _mm_tn(kw, v_h)
            n_scr[h:h + 1, :] = decay * n_h + jnp.sum(kw, axis=0, keepdims=True)
            m_row = jnp.where(mlane == h, m_new, m_row)

            hn = _rms(hh, mg_ref[:, cs])
            hg_scr[r0:r0 + chunk, cs] = hn * og[r0:r0 + chunk, cs]
            if h < M_HEADS - 1:
                yield
        m_scr[...] = m_row
        yield

    y = _mm(hg_scr[...], wob_ref[...])
    return jax.nn.sigmoid(gg + bg_ref[...]) * y


def _seg_sum_sq(x, seg01):
    sq = x * x
    hi = sq.astype(BF16)
    lo = (sq - hi.astype(F32)).astype(BF16)
    return jnp.dot(hi, seg01, preferred_element_type=F32) + jnp.dot(lo, seg01, preferred_element_type=F32)


def _seg_mean_sq(x, seg_ones):
    return _seg_sum_sq(x, seg_ones) * (1.0 / A_HEAD_DIM)


def _rope(x, cos, sin_signed):
    width = x.shape[1]
    half = A_HEAD_DIM // 2
    lane = lax.broadcasted_iota(jnp.int32, x.shape, 1)
    partner = jnp.where((lane % A_HEAD_DIM) < half,
                        pltpu.roll(x, width - half, axis=1),
                        pltpu.roll(x, half, axis=1))
    reps = width // cos.shape[1]
    if reps > 1:
        cos = jnp.concatenate([cos] * reps, axis=1)
        sin_signed = jnp.concatenate([sin_signed] * reps, axis=1)
    return x * cos + partner * sin_signed


def _attn_init(k0_ref, v0_ref, k_scr, v_scr):
    P, kvw = WINDOW, k_scr.shape[1]
    if k0_ref is None:
        k_scr[0:P, :] = jnp.zeros((P, kvw), F32)
        v_scr[0:P, :] = jnp.zeros((P, kvw), F32)
    else:
        k_scr[0:P, :] = k0_ref[0]
        v_scr[0:P, :] = v0_ref[0]


def _attn_section(u, t, sinks_ref, wq_ref, wkv_ref, wgg_ref, qg_ref, kg_ref, segred_ref, segexp_ref, segk_ref,
                  cos_ref, sin_ref, bg_ref, woc_ref, kn_ref, vn_ref,
                  k_scr, v_scr, o_scr, *, tile, valid, has_history):
    P = WINDOW
    qc = ATT_CHUNK
    kvw = k_scr.shape[1]
    groups_pairs = (wq_ref.shape[1] // kvw) // A_KV_HEADS

    aq = _mm(u, wq_ref[...])
    yield
    akv = _mm(u, wkv_ref[...])
    ak, av = akv[:, :kvw], akv[:, kvw:]
    cos = cos_ref[...]
    sin = sin_ref[...]
    kn = ak * lax.rsqrt(_seg_mean_sq(ak, segk_ref[...]) + EPS) * kg_ref[...]
    k_scr[P:P + tile, :] = _rope(kn, cos, sin)
    v_scr[P:P + tile, :] = av
    yield
    ms = _seg_sum_sq(aq, segred_ref[...]) * (1.0 / A_HEAD_DIM)
    qn = aq * _mm_split3(lax.rsqrt(ms + EPS), segexp_ref[...]) * qg_ref[...]
    qr = (_rope(qn, cos, sin) * (A_HEAD_DIM ** -0.5)).astype(BF16)
    yield
    gg = _mm(u, wgg_ref[...])
    yield

    lane_kv = lax.broadcasted_iota(jnp.int32, (P + qc, kvw), 1)
    lo_half = lane_kv < A_HEAD_DIM
    key_lane = lax.broadcasted_iota(jnp.int32, (1, 2 * P), 1)
    row4 = lax.broadcasted_iota(jnp.int32, (groups_pairs * qc, 1), 0)
    out_lane = lax.broadcasted_iota(jnp.int32, (1, kvw), 1)
    zpad = jnp.zeros((2 * P - (P + qc), kvw), BF16)

    for c in range(tile // qc):
        r0 = c * qc
        own = min(max(valid - r0, 0), qc)
        if own == 0:
            o_scr[r0:r0 + qc, :] = jnp.zeros((qc, o_scr.shape[1]), F32)
            continue
        if has_history:
            first_valid = 0
        else:
            first_valid = P - jnp.minimum(t * tile + r0, P)
        key_ok = (key_lane >= first_valid) & (key_lane < P + own)

        k_win = k_scr[r0:r0 + P + qc, :]
        v_win = v_scr[r0:r0 + P + qc, :]
        k_rot = pltpu.roll(k_win, A_HEAD_DIM, axis=1)
        v_rot = pltpu.roll(v_win, A_HEAD_DIM, axis=1)
        for kvh in range(A_KV_HEADS):
            k_a, k_b = (k_win, k_rot) if kvh == 0 else (k_rot, k_win)
            v_a, v_b = (v_win, v_rot) if kvh == 0 else (v_rot, v_win)
            k_bd = jnp.concatenate([jnp.where(lo_half, k_a, 0.0).astype(BF16), zpad,
                                    jnp.where(lo_half, 0.0, k_b).astype(BF16), zpad], axis=0)
            v_bd = jnp.concatenate([jnp.where(lo_half, v_a, 0.0).astype(BF16), zpad,
                                    jnp.where(lo_half, 0.0, v_b).astype(BF16), zpad], axis=0)
            pair0 = kvh * groups_pairs
            q_st = jnp.concatenate([qr[r0:r0 + qc, (pair0 + j) * kvw:(pair0 + j + 1) * kvw]
                                    for j in range(groups_pairs)], axis=0)
            s = _mm_nt(q_st, k_bd)
            probs, inv = [], []
            for half in range(2):
                s_h = jnp.where(key_ok, s[:, half * 2 * P:(half + 1) * 2 * P], -jnp.inf)
                sink = jnp.zeros((groups_pairs * qc, 1), F32)
                for j in range(groups_pairs):
                    sink = jnp.where((row4 >= j * qc) & (row4 < (j + 1) * qc),
                                     sinks_ref[(pair0 + j) * 2 + half], sink)
                mx = jnp.maximum(jnp.max(s_h, axis=1, keepdims=True), sink)
                p = jnp.exp(s_h - mx)
                probs.append(p.astype(BF16))
                inv.append(1.0 / (jnp.sum(p, axis=1, keepdims=True) + jnp.exp(sink - mx)))
            o = jnp.dot(jnp.concatenate(probs, axis=1), v_bd, preferred_element_type=F32)
            o = o * jnp.where(out_lane < A_HEAD_DIM, inv[0], inv[1])
            for j in range(groups_pairs):
                o_scr[r0:r0 + qc, (pair0 + j) * kvw:(pair0 + j + 1) * kvw] = o[j * qc:(j + 1) * qc, :]
            yield

    y = _mm(o_scr[...], woc_ref[...])

    k_new = k_scr[valid:valid + P, :]
    v_new = v_scr[valid:valid + P, :]
    k_scr[0:P, :] = k_new
    v_scr[0:P, :] = v_new
    kn_ref[0] = k_new
    vn_ref[0] = v_new
    return jax.nn.sigmoid(gg + bg_ref[...]) * y


N_LRU_W, N_MLSTM_W, N_ATTN_W = 11, 10, 12
N_STATE = 7


def _interleave(sections):
    results = [None] * len(sections)
    live = list(range(len(sections)))
    while live:
        for idx in list(live):
            try:
                next(sections[idx])
            except StopIteration as done:
                results[idx] = done.value
                live.remove(idx)
    return results


def _mixer_kernel(*refs, tile, chunk, valid, has_history):
    it = iter(refs)
    take = lambda n: [next(it) for _ in range(n)]
    sinks_ref, x_ref, g1_ref = take(3)
    lru_w, mlstm_w, attn_w = take(N_LRU_W), take(N_MLSTM_W), take(N_ATTN_W)
    (wout_ref,) = take(1)
    st_in = take(N_STATE) if has_history else [None] * N_STATE
    (y_ref,) = take(1)
    st_out = take(N_STATE)
    xpad_scr, h_scr, lh_scr, c_scr, n_scr, m_scr, q_scr, k_scr, v_scr, hg_scr, ak_scr, av_scr, o_scr = take(13)

    t = pl.program_id(1)
    nt = pl.num_programs(1)

    @pl.when(t == 0)
    def _():
        _lru_init(st_in[0], st_in[1], xpad_scr, h_scr)
        _mlstm_init(st_in[2], st_in[3], st_in[4], c_scr, n_scr, m_scr)
        _attn_init(st_in[5], st_in[6], ak_scr, av_scr)

    x = x_ref[0]
    u = _rms(x, g1_ref[...]).astype(BF16)
    y_a, y_b, y_c = _interleave([
        _lru_section(u, *lru_w, st_out[0], st_out[1], xpad_scr, h_scr, lh_scr, tile=tile, valid=valid),
        _mlstm_section(u, *mlstm_w, c_scr, n_scr, m_scr, q_scr, k_scr, v_scr, hg_scr,
                       tile=tile, chunk=chunk, valid=valid),
        _attn_section(u, t, sinks_ref, *attn_w, st_out[5], st_out[6], ak_scr, av_scr, o_scr,
                      tile=tile, valid=valid, has_history=has_history)])
    y_ref[0] = x + _mm((y_a + y_b) + y_c, wout_ref[...])

    @pl.when(t == nt - 1)
    def _():
        st_out[2][0] = c_scr[...]
        st_out[3][0] = n_scr[...]
        st_out[4][0] = m_scr[...]


def _mixer_call(sinks, x, g1, lru_w, mlstm_w, attn_w, wout, state, *, tile, valid):
    B, S, D = x.shape
    W = lru_w[0].shape[1]
    MW = mlstm_w[0].shape[1]
    AW = attn_w[0].shape[1]
    KW = A_KV_HEADS * A_HEAD_DIM
    hd = MW // M_HEADS
    nt = S // tile
    chunk = min(MLSTM_CHUNK, tile)
    has_history = state is not None
    assert nt * tile == S and tile % chunk == 0 and tile % ATT_CHUNK == 0 and (valid == tile or nt == 1)
    assert len(lru_w) == N_LRU_W and len(mlstm_w) == N_MLSTM_W and len(attn_w) == N_ATTN_W

    def const(a):
        return pl.BlockSpec(a.shape, lambda b, t, _n=a.ndim: (0,) * _n, pipeline_mode=pl.Buffered(1))

    def per_batch(shape):
        return pl.BlockSpec((1,) + shape[1:], lambda b, t, _n=len(shape): (b,) + (0,) * (_n - 1))

    cos, sin = attn_w[8], attn_w[9]
    attn_specs = [const(a) for a in attn_w]
    attn_specs[8] = pl.BlockSpec((tile, KW), lambda b, t: (t, 0))
    attn_specs[9] = pl.BlockSpec((tile, KW), lambda b, t: (t, 0))
    assert cos.shape == (S, KW) and sin.shape == (S, KW)

    state_shapes = [(B, CONV_W - 1, W), (B, 1, W), (B, M_HEADS, hd, hd), (B, M_HEADS, hd), (B, 1, LANES),
                    (B, WINDOW, KW), (B, WINDOW, KW)]
    in_specs = ([pl.BlockSpec(memory_space=pltpu.SMEM),
                 pl.BlockSpec((1, tile, D), lambda b, t: (b, t, 0)), const(g1)]
                + [const(a) for a in lru_w] + [const(a) for a in mlstm_w] + attn_specs + [const(wout)])
    args = [sinks, x, g1, *lru_w, *mlstm_w, *attn_w, wout]
    if has_history:
        in_specs += [per_batch(s) for s in state_shapes]
        args += list(state)
    kern = functools.partial(_mixer_kernel, tile=tile, chunk=chunk, valid=valid, has_history=has_history)
    return pl.pallas_call(
        kern,
        grid=(B, nt),
        in_specs=in_specs,
        out_specs=[pl.BlockSpec((1, tile, D), lambda b, t: (b, t, 0))] + [per_batch(s) for s in state_shapes],
        out_shape=[jax.ShapeDtypeStruct((B, S, D), F32)] + [jax.ShapeDtypeStruct(s, F32) for s in state_shapes],
        scratch_shapes=[
            pltpu.VMEM((tile + SUBLANES, W), F32), pltpu.VMEM((1, W), F32), pltpu.VMEM((tile, W), F32),
            pltpu.VMEM((M_HEADS, hd, hd), F32), pltpu.VMEM((M_HEADS, hd), F32), pltpu.VMEM((1, LANES), F32),
            pltpu.VMEM((tile, MW), F32), pltpu.VMEM((tile, MW), F32), pltpu.VMEM((tile, MW), F32),
            pltpu.VMEM((tile, MW), F32),
            pltpu.VMEM((WINDOW + tile, KW), F32), pltpu.VMEM((WINDOW + tile, KW), F32),
            pltpu.VMEM((tile, AW), F32),
        ],
        compiler_params=pltpu.CompilerParams(dimension_semantics=("arbitrary", "arbitrary"),
                                             vmem_limit_bytes=VMEM_LIMIT),
        name="mixer",
    )(*args)


def _mlp_kernel(x_ref, g2_ref, wup_ref, wdown_ref, o_ref):
    x = x_ref[...]
    up = _mm(_rms(x, g2_ref[...]), wup_ref[...])
    o_ref[...] = x + _mm(jnp.square(jax.nn.relu(up)), wdown_ref[...])


def _mlp_call(x, g2, wup, wdown, *, tile):
    N, D = x.shape
    FF = wup.shape[1]
    assert N % tile == 0
    row = lambda i: (i, 0)
    const = lambda shape: pl.BlockSpec(shape, lambda i: (0, 0), pipeline_mode=pl.Buffered(1))
    return pl.pallas_call(
        _mlp_kernel,
        grid=(N // tile,),
        in_specs=[pl.BlockSpec((tile, D), row), const((1, D)), const((D, FF)), const((FF, D))],
        out_specs=pl.BlockSpec((tile, D), row),
        out_shape=jax.ShapeDtypeStruct((N, D), F32),
        compiler_params=pltpu.CompilerParams(dimension_semantics=("arbitrary",),
                                             vmem_limit_bytes=VMEM_LIMIT),
        name="mlp",
    )(x, g2, wup, wdown)


def _rope_tables(pos, lanes):
    half = A_HEAD_DIM // 2
    inv = ROPE_THETA ** (-jnp.arange(half, dtype=F32) / half)
    ang = pos.astype(F32)[:, None] * inv[None, :]
    cos, sin = jnp.cos(ang), jnp.sin(ang)
    reps = lanes // A_HEAD_DIM
    return (jnp.tile(jnp.concatenate([cos, cos], axis=1), (1, reps)),
            jnp.tile(jnp.concatenate([-sin, sin], axis=1), (1, reps)))


def _seg_ones(width):
    seg = jnp.arange(width) // A_HEAD_DIM
    return (seg[:, None] == seg[None, :]).astype(BF16)


def _seg_select(width):
    seg = jnp.arange(width) // A_HEAD_DIM
    return (seg[:, None] == jnp.arange(LANES)[None, :]).astype(BF16)


def _paired_gate_weights(wa, wx):
    nb, blk, _ = wa.shape
    per = MXU_DIM // blk
    eye = jnp.eye(per, dtype=wa.dtype)[None, :, None, :, None]

    def block_diag(w):
        return (w.reshape(nb // per, per, blk, 1, blk) * eye).reshape(nb // per, per * blk, per * blk)

    return jnp.concatenate([block_diag(wa), block_diag(wx)], axis=-1).astype(BF16)


def _layer_weights(w):
    (norm1_g, w_in, conv_w, conv_b, lru_wa, lru_ba, lru_wx, lru_bx, lru_lam, m_bi, m_bf, m_norm_g, qn_g, kn_g,
     sinks, w_oa, w_ob, w_oc, b_gate, w_out, norm2_g, w_up, w_down) = w
    D = w_in.shape[0]
    W = conv_w.shape[1]
    MW = m_norm_g.shape[0]
    AW = w_oc.shape[0]
    KW = A_KV_HEADS * A_HEAD_DIM
    sizes = (W, W, MW, MW, MW, MW, M_HEADS, M_HEADS, AW, KW, KW, D, D, D)
    offs = [0]
    for s_ in sizes:
        offs.append(offs[-1] + s_)
    col = lambda i: w_in[:, offs[i]:offs[i + 1]].astype(BF16)
    row = lambda v: v.reshape(1, -1)

    lru_w = [col(0), col(1), col(11), conv_w, row(conv_b), _paired_gate_weights(lru_wa, lru_wx), row(lru_ba),
             row(lru_bx), row(lru_lam), row(b_gate[0]), w_oa.astype(BF16)]
    wif = jnp.zeros((D, LANES), BF16).at[:, :2 * M_HEADS].set(w_in[:, offs[6]:offs[8]].astype(BF16))
    bif = jnp.zeros((1, LANES), F32).at[0, :2 * M_HEADS].set(jnp.concatenate([m_bi, m_bf]))
    mlstm_w = [col(2), col(3), col(4), col(5), wif, bif, row(m_norm_g), col(12), row(b_gate[1]),
               w_ob.astype(BF16)]
    seg_red = _seg_select(AW)
    attn_w = [col(8), w_in[:, offs[9]:offs[11]].astype(BF16), col(13), row(jnp.tile(qn_g, AW // A_HEAD_DIM)),
              row(jnp.tile(kn_g, A_KV_HEADS)), seg_red, seg_red.T, _seg_ones(KW), None, None, row(b_gate[2]),
              w_oc.astype(BF16)]
    return (sinks, row(norm1_g), lru_w, mlstm_w, attn_w, w_out.astype(BF16),
            row(norm2_g), w_up.astype(BF16), w_down.astype(BF16))


def _layer(x, state, lw, cos, sin, *, tile, valid):
    sinks, g1, lru_w, mlstm_w, attn_w, wout, g2, wup, wdown = lw
    B, S, D = x.shape
    attn_w = attn_w[:8] + [cos, sin] + attn_w[10:]
    if S < tile:
        x = jnp.pad(x, ((0, 0), (0, tile - S), (0, 0)))
    outs = _mixer_call(sinks, x, g1, lru_w, mlstm_w, attn_w, wout, state, tile=tile, valid=valid)
    x1, new_state = outs[0][:, :S], outs[1:]
    N = B * S
    x2 = _mlp_call(x1.reshape(N, D), g2, wup, wdown, tile=min(MLP_TILE, N)).reshape(B, S, D)
    return x2, new_state


def kernel(x_prompt, x_sample, state_conv, state_lru, state_mlstm_C, state_mlstm_n, state_mlstm_m, cache_k, cache_v, norm1_g, w_in, conv_w, conv_b, lru_wa, lru_ba, lru_wx, lru_bx, lru_lam, m_bi, m_bf, m_norm_g, qn_g, kn_g, sinks, w_oa, w_ob, w_oc, b_gate, w_out, norm2_g, w_up, w_down):
    B, S, D = x_prompt.shape
    Bs, Sd, _ = x_sample.shape
    depth = w_in.shape[0]
    W = conv_w.shape[-1]
    KW = A_KV_HEADS * A_HEAD_DIM
    p_tile = min(PROMPT_TILE, S)
    assert S % p_tile == 0 and Sd <= SAMPLE_TILE and cache_k.shape[2] == WINDOW

    cos_p, sin_p = _rope_tables(jnp.arange(S, dtype=jnp.int32), KW)
    cos_s, sin_s = _rope_tables(PAST_LEN + jnp.arange(SAMPLE_TILE, dtype=jnp.int32), KW)

    y_p = x_prompt
    y_s = x_sample
    new_p, new_s = [], []
    for l in range(depth):
        lw = _layer_weights((norm1_g[l], w_in[l], conv_w[l], conv_b[l], lru_wa[l], lru_ba[l], lru_wx[l], lru_bx[l],
                             lru_lam[l], m_bi[l], m_bf[l], m_norm_g[l], qn_g[l], kn_g[l], sinks[l], w_oa[l], w_ob[l],
                             w_oc[l], b_gate[l], w_out[l], norm2_g[l], w_up[l], w_down[l]))
        y_p, sp = _layer(y_p, None, lw, cos_p, sin_p, tile=p_tile, valid=p_tile)
        m0 = jnp.zeros((Bs, 1, LANES), F32).at[:, 0, :M_HEADS].set(state_mlstm_m[l])
        st = (state_conv[l], state_lru[l].reshape(Bs, 1, W), state_mlstm_C[l], state_mlstm_n[l], m0,
              cache_k[l].reshape(Bs, WINDOW, KW), cache_v[l].reshape(Bs, WINDOW, KW))
        y_s, ss = _layer(y_s, st, lw, cos_s, sin_s, tile=SAMPLE_TILE, valid=Sd)
        new_p.append(sp)
        new_s.append(ss)

    def unpack(states):
        conv, h, c, n, m, k, v = [jnp.stack(s) for s in zip(*states)]
        nb = conv.shape[1]
        return (conv, h.reshape(depth, nb, W), c, n, m[:, :, 0, :M_HEADS],
                k.reshape(depth, nb, WINDOW, A_KV_HEADS, A_HEAD_DIM),
                v.reshape(depth, nb, WINDOW, A_KV_HEADS, A_HEAD_DIM))

    return (y_p, y_s, *unpack(new_p), *unpack(new_s))
```

```python
import functools

import jax
import jax.numpy as jnp
from jax import lax
from jax.experimental import pallas as pl
from jax.experimental.pallas import tpu as pltpu

F32 = jnp.float32
BF16 = jnp.bfloat16

EPS = 1e-6
CONV_W = 4
LRU_BLOCKS = 8
LRU_C = 8.0
M_HEADS = 4
A_HEAD_DIM = 64
A_KV_HEADS = 2
ATT_CHUNK = 64
WINDOW = 128
PAST_LEN = 2048
ROPE_THETA = 10000.0

SUBLANES = 8
LANES = 128
MXU_DIM = 256
MLSTM_CHUNK = 256
LRU_ROWS = 32
PROMPT_TILE = 256
SAMPLE_TILE = 128
MLP_TILE = 512
VMEM_LIMIT = 60 * 1024 * 1024


def _mm(a, b):
    return jnp.dot(a.astype(BF16), b.astype(BF16), preferred_element_type=F32)


def _mm_nt(a, b):
    return lax.dot_general(a.astype(BF16), b.astype(BF16), (((1,), (1,)), ((), ())),
                           preferred_element_type=F32)


def _mm_tn(a, b):
    return lax.dot_general(a.astype(BF16), b.astype(BF16), (((0,), (0,)), ((), ())),
                           preferred_element_type=F32)


def _mm_split3(a, b01):
    hi = a.astype(BF16)
    rest = a - hi.astype(F32)
    mid = rest.astype(BF16)
    lo = (rest - mid.astype(F32)).astype(BF16)
    dot = lambda p: jnp.dot(p, b01, preferred_element_type=F32)
    return (dot(hi) + dot(mid)) + dot(lo)


def _cumsum_lanes(x, triu01):
    return _mm_split3(x, triu01)


def _rms(x, g):
    ms = jnp.mean(x * x, axis=-1, keepdims=True)
    return (x * lax.rsqrt(ms + EPS)) * g


def _linear_scan(a, b, h0):
    rows, width = a.shape
    sub = lax.broadcasted_iota(jnp.int32, (SUBLANES, width), 0)
    out = []
    carry = h0
    for g in range(rows // SUBLANES):
        a_g = a[g * SUBLANES:(g + 1) * SUBLANES, :]
        b_g = b[g * SUBLANES:(g + 1) * SUBLANES, :]
        d = 1
        while d < SUBLANES:
            m = sub >= d
            b_g = jnp.where(m, a_g * pltpu.roll(b_g, d, axis=0) + b_g, b_g)
            a_g = jnp.where(m, a_g * pltpu.roll(a_g, d, axis=0), a_g)
            d *= 2
        h_g = b_g + a_g * carry
        out.append(h_g)
        carry = h_g[SUBLANES - 1:SUBLANES, :]
    return jnp.concatenate(out, axis=0)


def _lru_init(conv0_ref, h0_ref, xpad_scr, h_scr):
    pad = SUBLANES
    if conv0_ref is None:
        xpad_scr[0:pad, :] = jnp.zeros((pad, xpad_scr.shape[1]), F32)
        h_scr[...] = jnp.zeros(h_scr.shape, F32)
    else:
        xpad_scr[pad - (CONV_W - 1):pad, :] = conv0_ref[0]
        h_scr[...] = h0_ref[0]


def _lru_section(u, wxa_ref, wga_ref, wgg_ref, convw_ref, convb_ref, wgate_ref, ba_ref, bx_ref, lam_ref,
                 bg_ref, woa_ref, convn_ref, hlast_ref, xpad_scr, h_scr, hg_scr, merge_pre, *, tile, valid):
    pad = SUBLANES
    ntail = CONV_W - 1

    xpad_scr[pad:pad + tile, :] = _mm(u, wxa_ref[...])
    base = pad - ntail
    xc = convb_ref[...] + xpad_scr[base:base + tile, :] * convw_ref[0:1, :]
    for j in range(1, CONV_W):
        xc = xc + xpad_scr[base + j:base + j + tile, :] * convw_ref[j:j + 1, :]
    yield

    xcb = xc.astype(BF16)
    pairs = wgate_ref.shape[0]
    gates = [jnp.dot(xcb[:, p * MXU_DIM:(p + 1) * MXU_DIM], wgate_ref[p], preferred_element_type=F32)
             for p in range(pairs)]
    r_pre = jnp.concatenate([g[:, :MXU_DIM] for g in gates], axis=1) + ba_ref[...]
    i_pre = jnp.concatenate([g[:, MXU_DIM:] for g in gates], axis=1) + bx_ref[...]
    yield
    ga = _mm(u, wga_ref[...])
    yield

    carry = h_scr[...]
    sp = jax.nn.softplus(-lam_ref[...])
    for r0 in range(0, tile, LRU_ROWS):
        rows = slice(r0, r0 + LRU_ROWS)
        if r0 >= valid:
            hg_scr[rows, :] = jnp.zeros((LRU_ROWS, hg_scr.shape[1]), F32)
            continue
        r = jax.nn.sigmoid(r_pre[rows])
        i = jax.nn.sigmoid(i_pre[rows])
        log_a = -LRU_C * r * sp
        a = jnp.exp(log_a)
        th = jnp.tanh(log_a)
        uu = jnp.sqrt((-2.0 * th) / (1.0 - th)) * (i * xc[rows])
        h = _linear_scan(a, uu, carry)
        carry = h[LRU_ROWS - 1:LRU_ROWS, :]
        if r0 < valid <= r0 + LRU_ROWS:
            h_last = h[valid - 1 - r0:valid - r0, :]
            h_scr[...] = h_last
            hlast_ref[0] = h_last
        hg_scr[rows, :] = h * jax.nn.gelu(ga[rows])
    yield

    y = _mm(hg_scr[...], woa_ref[...])
    tail = xpad_scr[pad + valid - ntail:pad + valid, :]
    xpad_scr[pad - ntail:pad, :] = tail
    convn_ref[0] = tail
    return jax.nn.sigmoid(_merge_pre(u, wgg_ref, merge_pre, "L") + bg_ref[...]) * y


def _mlstm_init(c0_ref, n0_ref, m0_ref, c_scr, n_scr, m_scr):
    if c0_ref is None:
        c_scr[...] = jnp.zeros(c_scr.shape, F32)
        n_scr[...] = jnp.zeros(n_scr.shape, F32)
        m_scr[...] = jnp.zeros(m_scr.shape, F32)
    else:
        c_scr[...] = c0_ref[0]
        n_scr[...] = n0_ref[0]
        m_scr[...] = m0_ref[0]


def _mlstm_section(u, wq_ref, wk_ref, wv_ref, wo_ref, wif_ref, bif_ref, mg_ref, wgg_ref, bg_ref, wob_ref,
                   c_scr, n_scr, m_scr, q_scr, k_scr, v_scr, hg_scr, merge_pre, *, tile, chunk, valid):
    hd = q_scr.shape[1] // M_HEADS
    blocks = [(c * chunk, min(max(valid - c * chunk, 0), chunk)) for c in range(tile // chunk)]

    gates = _mm(u, wif_ref[...]) + bif_ref[...]
    glane = lax.broadcasted_iota(jnp.int32, gates.shape, 1)
    gates = jnp.where((glane >= M_HEADS) & (glane < 2 * M_HEADS), jax.nn.log_sigmoid(gates), gates)
    yield
    q_scr[...] = _mm(u, wq_ref[...])
    yield
    k_scr[...] = _mm(u, wk_ref[...]) * (hd ** -0.5)
    yield
    v_scr[...] = _mm(u, wv_ref[...])
    yield

    rr = lax.broadcasted_iota(jnp.int32, (chunk, chunk), 0)
    cc = lax.broadcasted_iota(jnp.int32, (chunk, chunk), 1)
    causal = cc <= rr
    triu = (rr <= cc).astype(BF16)
    grow = 2 * SUBLANES
    mlane = lax.broadcasted_iota(jnp.int32, (1, LANES), 1)
    ccol = lax.broadcasted_iota(jnp.int32, (1, chunk), 1)
    crow = lax.broadcasted_iota(jnp.int32, (chunk, 1), 0)

    cums = []
    for r0, vl in blocks:
        g_c = gates[r0:r0 + chunk, :]
        g_rows = g_c.T[0:grow, :]
        b_rows = _cumsum_lanes(g_rows, triu)
        b_all = jnp.concatenate([b_rows, jnp.zeros((LANES - grow, chunk), F32)], axis=0).T
        cums.append((g_c, g_rows, b_rows, b_all))
    yield
    og = None

    for (r0, vl), (g_c, g_rows, b_rows, b_all) in zip(blocks, cums):
        if vl == 0:
            hg_scr[r0:r0 + chunk, :] = jnp.zeros((chunk, hg_scr.shape[1]), F32)
            continue
        m_row = m_scr[...]
        heads = []
        for h in range(M_HEADS):
            cs = slice(h * hd, (h + 1) * hd)
            q_h = q_scr[r0:r0 + chunk, cs]
            k_h = k_scr[r0:r0 + chunk, cs]
            b_col = b_all[:, M_HEADS + h:M_HEADS + h + 1]
            b_row = b_rows[M_HEADS + h:M_HEADS + h + 1, :]
            ig_row = g_rows[h:h + 1, :]
            m_prev = m_row[:, h:h + 1]
            log_d = jnp.where(causal, (b_col - b_row) + ig_row, -jnp.inf)
            inter = b_col + m_prev
            m_t = jnp.maximum(inter, jnp.max(log_d, axis=1, keepdims=True))
            s = _mm_nt(q_h, k_h)
            qc = _mm(q_h, c_scr[h])
            heads.append((q_h, k_h, b_col, b_row, ig_row, m_prev, log_d, inter, m_t, s, qc))
            yield
        if og is None:
            og = jax.nn.sigmoid(_mm(u, wo_ref[...]))
            yield
        for h in range(M_HEADS):
            cs = slice(h * hd, (h + 1) * hd)
            q_h, k_h, b_col, b_row, ig_row, m_prev, log_d, inter, m_t, s, qc = heads[h]
            v_h = v_scr[r0:r0 + chunk, cs]
            ig_col = g_c[:, h:h + 1]
            c_h = c_scr[h]
            n_h = n_scr[h:h + 1, :]
            w = s * jnp.exp(log_d - m_t)
            inter_w = jnp.exp(inter - m_t)
            num = _mm(w, v_h) + inter_w * qc
            den = jnp.sum(w, axis=1, keepdims=True) + inter_w * jnp.sum(q_h * n_h, axis=1, keepdims=True)
            hh = num / jnp.maximum(jnp.abs(den), jnp.exp(-m_t))

            b_last = b_col[vl - 1:vl, :]
            tail_row = (b_last - b_row) + ig_row
            tail_col = (b_last - b_col) + ig_col
            if vl < chunk:
                tail_row = jnp.where(ccol < vl, tail_row, -jnp.inf)
                tail_col = jnp.where(crow < vl, tail_col, -jnp.inf)
            m_new = jnp.maximum(b_last + m_prev, jnp.max(tail_row, axis=1, keepdims=True))
            decay = jnp.exp((b_last + m_prev) - m_new)
            kw = k_h * jnp.exp(tail_col - m_new)
            c_scr[h] = decay * c_h + _mm_tn(kw, v_h)
            n_scr[h:h + 1, :] = decay * n_h + jnp.sum(kw, axis=0, keepdims=True)
            m_row = jnp.where(mlane == h, m_new, m_row)

            hn = _rms(hh, mg_ref[:, cs])
            hg_scr[r0:r0 + chunk, cs] = hn * og[r0:r0 + chunk, cs]
            if h == M_HEADS - 1:
                m_scr[...] = m_row
            yield

    y = _mm(hg_scr[...], wob_ref[...])
    return jax.nn.sigmoid(_merge_pre(u, wgg_ref, merge_pre, "M") + bg_ref[...]) * y


def _seg_sum_sq(x, seg01):
    sq = x * x
    hi = sq.astype(BF16)
    lo = (sq - hi.astype(F32)).astype(BF16)
    return jnp.dot(hi, seg01, preferred_element_type=F32) + jnp.dot(lo, seg01, preferred_element_type=F32)


def _seg_mean_sq(x, seg_ones):
    return _seg_sum_sq(x, seg_ones) * (1.0 / A_HEAD_DIM)


def _rope(x, cos, sin_signed):
    width = x.shape[1]
    half = A_HEAD_DIM // 2
    lane = lax.broadcasted_iota(jnp.int32, x.shape, 1)
    partner = jnp.where((lane % A_HEAD_DIM) < half,
                        pltpu.roll(x, width - half, axis=1),
                        pltpu.roll(x, half, axis=1))
    reps = width // cos.shape[1]
    if reps > 1:
        cos = jnp.concatenate([cos] * reps, axis=1)
        sin_signed = jnp.concatenate([sin_signed] * reps, axis=1)
    return x * cos + partner * sin_signed


def _attn_init(k0_ref, v0_ref, k_scr, v_scr):
    P, kvw = WINDOW, k_scr.shape[1]
    if k0_ref is None:
        k_scr[0:P, :] = jnp.zeros((P, kvw), F32)
        v_scr[0:P, :] = jnp.zeros((P, kvw), F32)
    else:
        k_scr[0:P, :] = k0_ref[0]
        v_scr[0:P, :] = v0_ref[0]


def _attn_section(u, t, sinks_ref, wq_ref, wkv_ref, wgg_ref, qg_ref, kg_ref, segred_ref, segexp_ref, segk_ref,
                  cos_ref, sin_ref, bg_ref, woc_ref, kn_ref, vn_ref,
                  k_scr, v_scr, o_scr, merge_pre, *, tile, valid, has_history):
    P = WINDOW
    qc = ATT_CHUNK
    kvw = k_scr.shape[1]
    groups_pairs = (wq_ref.shape[1] // kvw) // A_KV_HEADS

    aq = _mm(u, wq_ref[...])
    yield
    akv = _mm(u, wkv_ref[...])
    ak, av = akv[:, :kvw], akv[:, kvw:]
    cos = cos_ref[...]
    sin = sin_ref[...]
    kn = ak * lax.rsqrt(_seg_mean_sq(ak, segk_ref[...]) + EPS) * kg_ref[...]
    k_scr[P:P + tile, :] = _rope(kn, cos, sin)
    v_scr[P:P + tile, :] = av
    yield
    ms = _seg_sum_sq(aq, segred_ref[...]) * (1.0 / A_HEAD_DIM)
    yield
    qn = aq * _mm_split3(lax.rsqrt(ms + EPS), segexp_ref[...]) * qg_ref[...]
    qr = (_rope(qn, cos, sin) * (A_HEAD_DIM ** -0.5)).astype(BF16)
    yield

    lane_kv = lax.broadcasted_iota(jnp.int32, (P + qc, kvw), 1)
    lo_half = lane_kv < A_HEAD_DIM
    key_lane = lax.broadcasted_iota(jnp.int32, (1, 2 * P), 1)
    row4 = lax.broadcasted_iota(jnp.int32, (groups_pairs * qc, 1), 0)
    out_lane = lax.broadcasted_iota(jnp.int32, (1, kvw), 1)
    zpad = jnp.zeros((2 * P - (P + qc), kvw), BF16)

    blocks = [(c * qc, min(max(valid - c * qc, 0), qc)) for c in range(tile // qc)]

    scores = {}
    for r0, own in blocks:
        if own == 0:
            o_scr[r0:r0 + qc, :] = jnp.zeros((qc, o_scr.shape[1]), F32)
            continue
        k_win = k_scr[r0:r0 + P + qc, :]
        k_rot = pltpu.roll(k_win, A_HEAD_DIM, axis=1)
        for kvh in range(A_KV_HEADS):
            k_a, k_b = (k_win, k_rot) if kvh == 0 else (k_rot, k_win)
            k_bd = jnp.concatenate([jnp.where(lo_half, k_a, 0.0).astype(BF16), zpad,
                                    jnp.where(lo_half, 0.0, k_b).astype(BF16), zpad], axis=0)
            pair0 = kvh * groups_pairs
            q_st = jnp.concatenate([qr[r0:r0 + qc, (pair0 + j) * kvw:(pair0 + j + 1) * kvw]
                                    for j in range(groups_pairs)], axis=0)
            scores[r0, kvh] = _mm_nt(q_st, k_bd)
            yield

        if has_history:
            first_valid = 0
        else:
            first_valid = P - jnp.minimum(t * tile + r0, P)
        key_ok = (key_lane >= first_valid) & (key_lane < P + own)
        v_win = v_scr[r0:r0 + P + qc, :]
        v_rot = pltpu.roll(v_win, A_HEAD_DIM, axis=1)
        for kvh in range(A_KV_HEADS):
            v_a, v_b = (v_win, v_rot) if kvh == 0 else (v_rot, v_win)
            v_bd = jnp.concatenate([jnp.where(lo_half, v_a, 0.0).astype(BF16), zpad,
                                    jnp.where(lo_half, 0.0, v_b).astype(BF16), zpad], axis=0)
            pair0 = kvh * groups_pairs
            s = scores[r0, kvh]
            probs, inv = [], []
            for half in range(2):
                s_h = jnp.where(key_ok, s[:, half * 2 * P:(half + 1) * 2 * P], -jnp.inf)
                sink = jnp.zeros((groups_pairs * qc, 1), F32)
                for j in range(groups_pairs):
                    sink = jnp.where((row4 >= j * qc) & (row4 < (j + 1) * qc),
                                     sinks_ref[(pair0 + j) * 2 + half], sink)
                mx = jnp.maximum(jnp.max(s_h, axis=1, keepdims=True), sink)
                p = jnp.exp(s_h - mx)
                probs.append(p.astype(BF16))
                inv.append(1.0 / (jnp.sum(p, axis=1, keepdims=True) + jnp.exp(sink - mx)))
            o = jnp.dot(jnp.concatenate(probs, axis=1), v_bd, preferred_element_type=F32)
            o = o * jnp.where(out_lane < A_HEAD_DIM, inv[0], inv[1])
            for j in range(groups_pairs):
                o_scr[r0:r0 + qc, (pair0 + j) * kvw:(pair0 + j + 1) * kvw] = o[j * qc:(j + 1) * qc, :]
            yield

    y = _mm(o_scr[...], woc_ref[...])

    k_new = k_scr[valid:valid + P, :]
    v_new = v_scr[valid:valid + P, :]
    k_scr[0:P, :] = k_new
    v_scr[0:P, :] = v_new
    kn_ref[0] = k_new
    vn_ref[0] = v_new
    return jax.nn.sigmoid(_merge_pre(u, wgg_ref, merge_pre, "A") + bg_ref[...]) * y


def _merge_pre(u, wgg_ref, merge_pre, key):
    if key not in merge_pre:
        merge_pre[key] = _mm(u, wgg_ref[...])
    return merge_pre[key]


def _merge_gate_section(u, wgg_refs, merge_pre):
    for key, wgg_ref in wgg_refs.items():
        _merge_pre(u, wgg_ref, merge_pre, key)
        yield


N_LRU_W, N_MLSTM_W, N_ATTN_W = 11, 10, 12
N_STATE = 7


PIECE_ORDER = (
    "LMAAM"
    "LMAMM"
    "ALGG"
    "MMMMM"
    "AAL"
    "G"
    "MMAAAAMMAAAA"
    "MAAAALAAA"
)


def _trace_in_order(sections, order):
    results = {}

    def step(key):
        try:
            next(sections[key])
        except StopIteration as done:
            results[key] = done.value

    for key in order:
        if key not in results:
            step(key)
    while len(results) < len(sections):
        for key in sections:
            if key not in results:
                step(key)
    return results


def _mixer_kernel(*refs, tile, chunk, valid, has_history):
    it = iter(refs)
    take = lambda n: [next(it) for _ in range(n)]
    sinks_ref, x_ref, g1_ref = take(3)
    lru_w, mlstm_w, attn_w = take(N_LRU_W), take(N_MLSTM_W), take(N_ATTN_W)
    (wout_ref,) = take(1)
    st_in = take(N_STATE) if has_history else [None] * N_STATE
    (y_ref,) = take(1)
    st_out = take(N_STATE)
    xpad_scr, h_scr, lh_scr, c_scr, n_scr, m_scr, q_scr, k_scr, v_scr, hg_scr, ak_scr, av_scr, o_scr = take(13)

    t = pl.program_id(1)
    nt = pl.num_programs(1)

    @pl.when(t == 0)
    def _():
        _lru_init(st_in[0], st_in[1], xpad_scr, h_scr)
        _mlstm_init(st_in[2], st_in[3], st_in[4], c_scr, n_scr, m_scr)
        _attn_init(st_in[5], st_in[6], ak_scr, av_scr)

    x = x_ref[0]
    u = _rms(x, g1_ref[...]).astype(BF16)
    merge_pre = {}
    y = _trace_in_order({
        "L": _lru_section(u, *lru_w, st_out[0], st_out[1], xpad_scr, h_scr, lh_scr, merge_pre,
                          tile=tile, valid=valid),
        "M": _mlstm_section(u, *mlstm_w, c_scr, n_scr, m_scr, q_scr, k_scr, v_scr, hg_scr, merge_pre,
                            tile=tile, chunk=chunk, valid=valid),
        "A": _attn_section(u, t, sinks_ref, *attn_w, st_out[5], st_out[6], ak_scr, av_scr, o_scr, merge_pre,
                           tile=tile, valid=valid, has_history=has_history),
        "G": _merge_gate_section(u, {"L": lru_w[2], "M": mlstm_w[7], "A": attn_w[2]}, merge_pre)}, PIECE_ORDER)
    y_ref[0] = x + _mm((y["L"] + y["M"]) + y["A"], wout_ref[...])

    @pl.when(t == nt - 1)
    def _():
        st_out[2][0] = c_scr[...]
        st_out[3][0] = n_scr[...]
        st_out[4][0] = m_scr[...]


def _mixer_call(sinks, x, g1, lru_w, mlstm_w, attn_w, wout, state, *, tile, valid):
    B, S, D = x.shape
    W = lru_w[0].shape[1]
    MW = mlstm_w[0].shape[1]
    AW = attn_w[0].shape[1]
    KW = A_KV_HEADS * A_HEAD_DIM
    hd = MW // M_HEADS
    nt = S // tile
    chunk = min(MLSTM_CHUNK, tile)
    has_history = state is not None
    assert nt * tile == S and tile % chunk == 0 and tile % ATT_CHUNK == 0 and (valid == tile or nt == 1)
    assert len(lru_w) == N_LRU_W and len(mlstm_w) == N_MLSTM_W and len(attn_w) == N_ATTN_W

    def const(a):
        return pl.BlockSpec(a.shape, lambda b, t, _n=a.ndim: (0,) * _n, pipeline_mode=pl.Buffered(1))

    def per_batch(shape):
        return pl.BlockSpec((1,) + shape[1:], lambda b, t, _n=len(shape): (b,) + (0,) * (_n - 1))

    cos, sin = attn_w[8], attn_w[9]
    attn_specs = [const(a) for a in attn_w]
    attn_specs[8] = pl.BlockSpec((tile, KW), lambda b, t: (t, 0))
    attn_specs[9] = pl.BlockSpec((tile, KW), lambda b, t: (t, 0))
    assert cos.shape == (S, KW) and sin.shape == (S, KW)

    state_shapes = [(B, CONV_W - 1, W), (B, 1, W), (B, M_HEADS, hd, hd), (B, M_HEADS, hd), (B, 1, LANES),
                    (B, WINDOW, KW), (B, WINDOW, KW)]
    in_specs = ([pl.BlockSpec(memory_space=pltpu.SMEM),
                 pl.BlockSpec((1, tile, D), lambda b, t: (b, t, 0)), const(g1)]
                + [const(a) for a in lru_w] + [const(a) for a in mlstm_w] + attn_specs + [const(wout)])
    args = [sinks, x, g1, *lru_w, *mlstm_w, *attn_w, wout]
    if has_history:
        in_specs += [per_batch(s) for s in state_shapes]
        args += list(state)
    kern = functools.partial(_mixer_kernel, tile=tile, chunk=chunk, valid=valid, has_history=has_history)
    return pl.pallas_call(
        kern,
        grid=(B, nt),
        in_specs=in_specs,
        out_specs=[pl.BlockSpec((1, tile, D), lambda b, t: (b, t, 0))] + [per_batch(s) for s in state_shapes],
        out_shape=[jax.ShapeDtypeStruct((B, S, D), F32)] + [jax.ShapeDtypeStruct(s, F32) for s in state_shapes],
        scratch_shapes=[
            pltpu.VMEM((tile + SUBLANES, W), F32), pltpu.VMEM((1, W), F32), pltpu.VMEM((tile, W), F32),
            pltpu.VMEM((M_HEADS, hd, hd), F32), pltpu.VMEM((M_HEADS, hd), F32), pltpu.VMEM((1, LANES), F32),
            pltpu.VMEM((tile, MW), F32), pltpu.VMEM((tile, MW), F32), pltpu.VMEM((tile, MW), F32),
            pltpu.VMEM((tile, MW), F32),
            pltpu.VMEM((WINDOW + tile, KW), F32), pltpu.VMEM((WINDOW + tile, KW), F32),
            pltpu.VMEM((tile, AW), F32),
        ],
        compiler_params=pltpu.CompilerParams(dimension_semantics=("arbitrary", "arbitrary"),
                                             vmem_limit_bytes=VMEM_LIMIT),
        name="mixer",
    )(*args)


def _mlp_kernel(x_ref, g2_ref, wup_ref, wdown_ref, o_ref):
    x = x_ref[...]
    up = _mm(_rms(x, g2_ref[...]), wup_ref[...])
    o_ref[...] = x + _mm(jnp.square(jax.nn.relu(up)), wdown_ref[...])


def _mlp_call(x, g2, wup, wdown, *, tile):
    N, D = x.shape
    FF = wup.shape[1]
    assert N % tile == 0
    row = lambda i: (i, 0)
    const = lambda shape: pl.BlockSpec(shape, lambda i: (0, 0), pipeline_mode=pl.Buffered(1))
    return pl.pallas_call(
        _mlp_kernel,
        grid=(N // tile,),
        in_specs=[pl.BlockSpec((tile, D), row), const((1, D)), const((D, FF)), const((FF, D))],
        out_specs=pl.BlockSpec((tile, D), row),
        out_shape=jax.ShapeDtypeStruct((N, D), F32),
        compiler_params=pltpu.CompilerParams(dimension_semantics=("arbitrary",),
                                             vmem_limit_bytes=VMEM_LIMIT),
        name="mlp",
    )(x, g2, wup, wdown)


def _rope_tables(pos, lanes):
    half = A_HEAD_DIM // 2
    inv = ROPE_THETA ** (-jnp.arange(half, dtype=F32) / half)
    ang = pos.astype(F32)[:, None] * inv[None, :]
    cos, sin = jnp.cos(ang), jnp.sin(ang)
    reps = lanes // A_HEAD_DIM
    return (jnp.tile(jnp.concatenate([cos, cos], axis=1), (1, reps)),
            jnp.tile(jnp.concatenate([-sin, sin], axis=1), (1, reps)))


def _seg_ones(width):
    seg = jnp.arange(width) // A_HEAD_DIM
    return (seg[:, None] == seg[None, :]).astype(BF16)


def _seg_select(width):
    seg = jnp.arange(width) // A_HEAD_DIM
    return (seg[:, None] == jnp.arange(LANES)[None, :]).astype(BF16)


def _paired_gate_weights(wa, wx):
    nb, blk, _ = wa.shape
    per = MXU_DIM // blk
    eye = jnp.eye(per, dtype=wa.dtype)[None, :, None, :, None]

    def block_diag(w):
        return (w.reshape(nb // per, per, blk, 1, blk) * eye).reshape(nb // per, per * blk, per * blk)

    return jnp.concatenate([block_diag(wa), block_diag(wx)], axis=-1).astype(BF16)


def _layer_weights(w):
    (norm1_g, w_in, conv_w, conv_b, lru_wa, lru_ba, lru_wx, lru_bx, lru_lam, m_bi, m_bf, m_norm_g, qn_g, kn_g,
     sinks, w_oa, w_ob, w_oc, b_gate, w_out, norm2_g, w_up, w_down) = w
    D = w_in.shape[0]
    W = conv_w.shape[1]
    MW = m_norm_g.shape[0]
    AW = w_oc.shape[0]
    KW = A_KV_HEADS * A_HEAD_DIM
    sizes = (W, W, MW, MW, MW, MW, M_HEADS, M_HEADS, AW, KW, KW, D, D, D)
    offs = [0]
    for s_ in sizes:
        offs.append(offs[-1] + s_)
    col = lambda i: w_in[:, offs[i]:offs[i + 1]].astype(BF16)
    row = lambda v: v.reshape(1, -1)

    lru_w = [col(0), col(1), col(11), conv_w, row(conv_b), _paired_gate_weights(lru_wa, lru_wx), row(lru_ba),
             row(lru_bx), row(lru_lam), row(b_gate[0]), w_oa.astype(BF16)]
    wif = jnp.zeros((D, LANES), BF16).at[:, :2 * M_HEADS].set(w_in[:, offs[6]:offs[8]].astype(BF16))
    bif = jnp.zeros((1, LANES), F32).at[0, :2 * M_HEADS].set(jnp.concatenate([m_bi, m_bf]))
    mlstm_w = [col(2), col(3), col(4), col(5), wif, bif, row(m_norm_g), col(12), row(b_gate[1]),
               w_ob.astype(BF16)]
    seg_red = _seg_select(AW)
    attn_w = [col(8), w_in[:, offs[9]:offs[11]].astype(BF16), col(13), row(jnp.tile(qn_g, AW // A_HEAD_DIM)),
              row(jnp.tile(kn_g, A_KV_HEADS)), seg_red, seg_red.T, _seg_ones(KW), None, None, row(b_gate[2]),
              w_oc.astype(BF16)]
    return (sinks, row(norm1_g), lru_w, mlstm_w, attn_w, w_out.astype(BF16),
            row(norm2_g), w_up.astype(BF16), w_down.astype(BF16))


def _layer(x, state, lw, cos, sin, *, tile, valid):
    sinks, g1, lru_w, mlstm_w, attn_w, wout, g2, wup, wdown = lw
    B, S, D = x.shape
    attn_w = attn_w[:8] + [cos, sin] + attn_w[10:]
    if S < tile:
        x = jnp.pad(x, ((0, 0), (0, tile - S), (0, 0)))
    outs = _mixer_call(sinks, x, g1, lru_w, mlstm_w, attn_w, wout, state, tile=tile, valid=valid)
    x1, new_state = outs[0][:, :S], outs[1:]
    N = B * S
    x2 = _mlp_call(x1.reshape(N, D), g2, wup, wdown, tile=min(MLP_TILE, N)).reshape(B, S, D)
    return x2, new_state


def kernel(x_prompt, x_sample, state_conv, state_lru, state_mlstm_C, state_mlstm_n, state_mlstm_m, cache_k, cache_v, norm1_g, w_in, conv_w, conv_b, lru_wa, lru_ba, lru_wx, lru_bx, lru_lam, m_bi, m_bf, m_norm_g, qn_g, kn_g, sinks, w_oa, w_ob, w_oc, b_gate, w_out, norm2_g, w_up, w_down):
    B, S, D = x_prompt.shape
    Bs, Sd, _ = x_sample.shape
    depth = w_in.shape[0]
    W = conv_w.shape[-1]
    KW = A_KV_HEADS * A_HEAD_DIM
    p_tile = min(PROMPT_TILE, S)
    assert S % p_tile == 0 and Sd <= SAMPLE_TILE and cache_k.shape[2] == WINDOW

    cos_p, sin_p = _rope_tables(jnp.arange(S, dtype=jnp.int32), KW)
    cos_s, sin_s = _rope_tables(PAST_LEN + jnp.arange(SAMPLE_TILE, dtype=jnp.int32), KW)

    y_p = x_prompt
    y_s = x_sample
    new_p, new_s = [], []
    for l in range(depth):
        lw = _layer_weights((norm1_g[l], w_in[l], conv_w[l], conv_b[l], lru_wa[l], lru_ba[l], lru_wx[l], lru_bx[l],
                             lru_lam[l], m_bi[l], m_bf[l], m_norm_g[l], qn_g[l], kn_g[l], sinks[l], w_oa[l], w_ob[l],
                             w_oc[l], b_gate[l], w_out[l], norm2_g[l], w_up[l], w_down[l]))
        y_p, sp = _layer(y_p, None, lw, cos_p, sin_p, tile=p_tile, valid=p_tile)
        m0 = jnp.zeros((Bs, 1, LANES), F32).at[:, 0, :M_HEADS].set(state_mlstm_m[l])
        st = (state_conv[l], state_lru[l].reshape(Bs, 1, W), state_mlstm_C[l], state_mlstm_n[l], m0,
              cache_k[l].reshape(Bs, WINDOW, KW), cache_v[l].reshape(Bs, WINDOW, KW))
        y_s, ss = _layer(y_s, st, lw, cos_s, sin_s, tile=SAMPLE_TILE, valid=Sd)
        new_p.append(sp)
        new_s.append(ss)

    def unpack(states):
        conv, h, c, n, m, k, v = [jnp.stack(s) for s in zip(*states)]
        nb = conv.shape[1]
        return (conv, h.reshape(depth, nb, W), c, n, m[:, :, 0, :M_HEADS],
                k.reshape(depth, nb, WINDOW, A_KV_HEADS, A_HEAD_DIM),
                v.reshape(depth, nb, WINDOW, A_KV_HEADS, A_HEAD_DIM))

    return (y_p, y_s, *unpack(new_p), *unpack(new_s))
```

```python
import functools

import jax
import jax.numpy as jnp
from jax import lax
from jax.experimental import pallas as pl
from jax.experimental.pallas import tpu as pltpu

F32 = jnp.float32
BF16 = jnp.bfloat16

EPS = 1e-6
CONV_W = 4
LRU_BLOCKS = 8
LRU_C = 8.0
M_HEADS = 4
A_HEAD_DIM = 64
A_KV_HEADS = 2
ATT_CHUNK = 64
WINDOW = 128
PAST_LEN = 2048
ROPE_THETA = 10000.0

SUBLANES = 8
LANES = 128
MXU_DIM = 256
MLSTM_CHUNK = 256
LRU_ROWS = 32
PROMPT_TILE = 256
SAMPLE_TILE = 128
MLP_TILE = 512
VMEM_LIMIT = 60 * 1024 * 1024


def _mm(a, b):
    return jnp.dot(a.astype(BF16), b.astype(BF16), preferred_element_type=F32)


def _mm_nt(a, b):
    return lax.dot_general(a.astype(BF16), b.astype(BF16), (((1,), (1,)), ((), ())),
                           preferred_element_type=F32)


def _mm_tn(a, b):
    return lax.dot_general(a.astype(BF16), b.astype(BF16), (((0,), (0,)), ((), ())),
                           preferred_element_type=F32)


def _mm_split3(a, b01):
    hi = a.astype(BF16)
    rest = a - hi.astype(F32)
    mid = rest.astype(BF16)
    lo = (rest - mid.astype(F32)).astype(BF16)
    dot = lambda p: jnp.dot(p, b01, preferred_element_type=F32)
    return (dot(hi) + dot(mid)) + dot(lo)


def _cumsum_lanes(x, triu01):
    return _mm_split3(x, triu01)


def _rms(x, g):
    ms = jnp.mean(x * x, axis=-1, keepdims=True)
    return (x * lax.rsqrt(ms + EPS)) * g


def _linear_scan(a, b, h0):
    rows, width = a.shape
    sub = lax.broadcasted_iota(jnp.int32, (SUBLANES, width), 0)
    out = []
    carry = h0
    for g in range(rows // SUBLANES):
        a_g = a[g * SUBLANES:(g + 1) * SUBLANES, :]
        b_g = b[g * SUBLANES:(g + 1) * SUBLANES, :]
        d = 1
        while d < SUBLANES:
            m = sub >= d
            b_g = jnp.where(m, a_g * pltpu.roll(b_g, d, axis=0) + b_g, b_g)
            a_g = jnp.where(m, a_g * pltpu.roll(a_g, d, axis=0), a_g)
            d *= 2
        h_g = b_g + a_g * carry
        out.append(h_g)
        carry = h_g[SUBLANES - 1:SUBLANES, :]
    return jnp.concatenate(out, axis=0)


def _lru_init(conv0_ref, h0_ref, xpad_scr, h_scr):
    pad = SUBLANES
    if conv0_ref is None:
        xpad_scr[0:pad, :] = jnp.zeros((pad, xpad_scr.shape[1]), F32)
        h_scr[...] = jnp.zeros(h_scr.shape, F32)
    else:
        xpad_scr[pad - (CONV_W - 1):pad, :] = conv0_ref[0]
        h_scr[...] = h0_ref[0]


def _lru_section(u, wxa_ref, wga_ref, wgg_ref, convw_ref, convb_ref, wgate_ref, ba_ref, bx_ref, lam_ref,
                 bg_ref, woa_ref, convn_ref, hlast_ref, xpad_scr, h_scr, hg_scr, merge_pre, *, tile, valid):
    pad = SUBLANES
    ntail = CONV_W - 1

    xpad_scr[pad:pad + tile, :] = _mm(u, wxa_ref[...])
    base = pad - ntail
    xc = convb_ref[...] + xpad_scr[base:base + tile, :] * convw_ref[0:1, :]
    for j in range(1, CONV_W):
        xc = xc + xpad_scr[base + j:base + j + tile, :] * convw_ref[j:j + 1, :]
    yield

    xcb = xc.astype(BF16)
    pairs = wgate_ref.shape[0]
    gates = [jnp.dot(xcb[:, p * MXU_DIM:(p + 1) * MXU_DIM], wgate_ref[p], preferred_element_type=F32)
             for p in range(pairs)]
    r_pre = jnp.concatenate([g[:, :MXU_DIM] for g in gates], axis=1) + ba_ref[...]
    i_pre = jnp.concatenate([g[:, MXU_DIM:] for g in gates], axis=1) + bx_ref[...]
    yield
    ga = _mm(u, wga_ref[...])
    yield

    carry = h_scr[...]
    sp = jax.nn.softplus(-lam_ref[...])
    for r0 in range(0, tile, LRU_ROWS):
        rows = slice(r0, r0 + LRU_ROWS)
        if r0 >= valid:
            hg_scr[rows, :] = jnp.zeros((LRU_ROWS, hg_scr.shape[1]), F32)
            continue
        r = jax.nn.sigmoid(r_pre[rows])
        i = jax.nn.sigmoid(i_pre[rows])
        log_a = -LRU_C * r * sp
        a = jnp.exp(log_a)
        th = jnp.tanh(log_a)
        uu = jnp.sqrt((-2.0 * th) / (1.0 - th)) * (i * xc[rows])
        h = _linear_scan(a, uu, carry)
        carry = h[LRU_ROWS - 1:LRU_ROWS, :]
        if r0 < valid <= r0 + LRU_ROWS:
            h_last = h[valid - 1 - r0:valid - r0, :]
            h_scr[...] = h_last
            hlast_ref[0] = h_last
        hg_scr[rows, :] = h * jax.nn.gelu(ga[rows])
    yield

    y = _mm(hg_scr[...], woa_ref[...])
    tail = xpad_scr[pad + valid - ntail:pad + valid, :]
    xpad_scr[pad - ntail:pad, :] = tail
    convn_ref[0] = tail
    return jax.nn.sigmoid(_merge_pre(u, wgg_ref, merge_pre, "L") + bg_ref[...]) * y


def _mlstm_init(c0_ref, n0_ref, m0_ref, c_scr, n_scr, m_scr):
    if c0_ref is None:
        c_scr[...] = jnp.zeros(c_scr.shape, F32)
        n_scr[...] = jnp.zeros(n_scr.shape, F32)
        m_scr[...] = jnp.zeros(m_scr.shape, F32)
    else:
        c_scr[...] = c0_ref[0]
        n_scr[...] = n0_ref[0]
        m_scr[...] = m0_ref[0]


def _mlstm_section(u, wq_ref, wk_ref, wv_ref, wo_ref, wif_ref, bif_ref, mg_ref, wgg_ref, bg_ref, wob_ref,
                   c_scr, n_scr, m_scr, q_scr, k_scr, v_scr, hg_scr, merge_pre, *, tile, chunk, valid):
    hd = q_scr.shape[1] // M_HEADS
    blocks = [(c * chunk, min(max(valid - c * chunk, 0), chunk)) for c in range(tile // chunk)]

    gates = _mm(u, wif_ref[...]) + bif_ref[...]
    glane = lax.broadcasted_iota(jnp.int32, gates.shape, 1)
    gates = jnp.where((glane >= M_HEADS) & (glane < 2 * M_HEADS), jax.nn.log_sigmoid(gates), gates)
    yield
    q_scr[...] = _mm(u, wq_ref[...])
    yield
    k_scr[...] = _mm(u, wk_ref[...]) * (hd ** -0.5)
    yield
    v_scr[...] = _mm(u, wv_ref[...])
    yield

    rr = lax.broadcasted_iota(jnp.int32, (chunk, chunk), 0)
    cc = lax.broadcasted_iota(jnp.int32, (chunk, chunk), 1)
    causal = cc <= rr
    triu = (rr <= cc).astype(BF16)
    grow = 2 * SUBLANES
    mlane = lax.broadcasted_iota(jnp.int32, (1, LANES), 1)
    ccol = lax.broadcasted_iota(jnp.int32, (1, chunk), 1)
    crow = lax.broadcasted_iota(jnp.int32, (chunk, 1), 0)

    cums = []
    for r0, vl in blocks:
        g_c = gates[r0:r0 + chunk, :]
        g_rows = g_c.T[0:grow, :]
        b_rows = _cumsum_lanes(g_rows, triu)
        b_all = jnp.concatenate([b_rows, jnp.zeros((LANES - grow, chunk), F32)], axis=0).T
        cums.append((g_c, g_rows, b_rows, b_all))
    yield
    og = None

    for (r0, vl), (g_c, g_rows, b_rows, b_all) in zip(blocks, cums):
        if vl == 0:
            hg_scr[r0:r0 + chunk, :] = jnp.zeros((chunk, hg_scr.shape[1]), F32)
            continue
        m_row = m_scr[...]
        heads = []
        for h in range(M_HEADS):
            cs = slice(h * hd, (h + 1) * hd)
            q_h = q_scr[r0:r0 + chunk, cs]
            k_h = k_scr[r0:r0 + chunk, cs]
            b_col = b_all[:, M_HEADS + h:M_HEADS + h + 1]
            b_row = b_rows[M_HEADS + h:M_HEADS + h + 1, :]
            ig_row = g_rows[h:h + 1, :]
            m_prev = m_row[:, h:h + 1]
            log_d = jnp.where(causal, (b_col - b_row) + ig_row, -jnp.inf)
            inter = b_col + m_prev
            m_t = jnp.maximum(inter, jnp.max(log_d, axis=1, keepdims=True))
            s = _mm_nt(q_h, k_h)
            qc = _mm(q_h, c_scr[h])
            heads.append((q_h, k_h, b_col, b_row, ig_row, m_prev, log_d, inter, m_t, s, qc))
            yield
        if og is None:
            og = jax.nn.sigmoid(_mm(u, wo_ref[...]))
            yield
        for h in range(M_HEADS):
            cs = slice(h * hd, (h + 1) * hd)
            q_h, k_h, b_col, b_row, ig_row, m_prev, log_d, inter, m_t, s, qc = heads[h]
            v_h = v_scr[r0:r0 + chunk, cs]
            ig_col = g_c[:, h:h + 1]
            c_h = c_scr[h]
            n_h = n_scr[h:h + 1, :]
            w = s * jnp.exp(log_d - m_t)
            inter_w = jnp.exp(inter - m_t)
            num = _mm(w, v_h) + inter_w * qc
            den = jnp.sum(w, axis=1, keepdims=True) + inter_w * jnp.sum(q_h * n_h, axis=1, keepdims=True)
            hh = num / jnp.maximum(jnp.abs(den), jnp.exp(-m_t))

            b_last = b_col[vl - 1:vl, :]
            tail_row = (b_last - b_row) + ig_row
            tail_col = (b_last - b_col) + ig_col
            if vl < chunk:
                tail_row = jnp.where(ccol < vl, tail_row, -jnp.inf)
                tail_col = jnp.where(crow < vl, tail_col, -jnp.inf)
            m_new = jnp.maximum(b_last + m_prev, jnp.max(tail_row, axis=1, keepdims=True))
            decay = jnp.exp((b_last + m_prev) - m_new)
            kw = k_h * jnp.exp(tail_col - m_new)
            c_scr[h] = decay * c_h + _mm_tn(kw, v_h)
            n_scr[h:h + 1, :] = decay * n_h + jnp.sum(kw, axis=0, keepdims=True)
            m_row = jnp.where(mlane == h, m_new, m_row)

            hn = _rms(hh, mg_ref[:, cs])
            hg_scr[r0:r0 + chunk, cs] = hn * og[r0:r0 + chunk, cs]
            if h == M_HEADS - 1:
                m_scr[...] = m_row
            yield

    y = _mm(hg_scr[...], wob_ref[...])
    return jax.nn.sigmoid(_merge_pre(u, wgg_ref, merge_pre, "M") + bg_ref[...]) * y


def _seg_sum_sq(x, seg01):
    sq = x * x
    hi = sq.astype(BF16)
    lo = (sq - hi.astype(F32)).astype(BF16)
    return jnp.dot(hi, seg01, preferred_element_type=F32) + jnp.dot(lo, seg01, preferred_element_type=F32)


def _seg_mean_sq(x, seg_ones):
    return _seg_sum_sq(x, seg_ones) * (1.0 / A_HEAD_DIM)


def _rope(x, cos, sin_signed):
    width = x.shape[1]
    half = A_HEAD_DIM // 2
    lane = lax.broadcasted_iota(jnp.int32, x.shape, 1)
    partner = jnp.where((lane % A_HEAD_DIM) < half,
                        pltpu.roll(x, width - half, axis=1),
                        pltpu.roll(x, half, axis=1))
    reps = width // cos.shape[1]
    if reps > 1:
        cos = jnp.concatenate([cos] * reps, axis=1)
        sin_signed = jnp.concatenate([sin_signed] * reps, axis=1)
    return x * cos + partner * sin_signed


def _attn_init(k0_ref, v0_ref, k_scr, v_scr):
    P, kvw = WINDOW, k_scr.shape[1]
    if k0_ref is None:
        k_scr[0:P, :] = jnp.zeros((P, kvw), F32)
        v_scr[0:P, :] = jnp.zeros((P, kvw), F32)
    else:
        k_scr[0:P, :] = k0_ref[0]
        v_scr[0:P, :] = v0_ref[0]


def _attn_section(u, t, sinks_ref, wq_ref, wkv_ref, wgg_ref, qg_ref, kg_ref, segred_ref, segk_ref,
                  cos_ref, sin_ref, bg_ref, woc_ref, kn_ref, vn_ref,
                  k_scr, v_scr, o_scr, merge_pre, *, tile, valid, has_history):
    P = WINDOW
    qc = ATT_CHUNK
    kvw = k_scr.shape[1]
    groups_pairs = (wq_ref.shape[1] // kvw) // A_KV_HEADS

    aq = _mm(u, wq_ref[...])
    yield
    akv = _mm(u, wkv_ref[...])
    ak, av = akv[:, :kvw], akv[:, kvw:]
    yield
    cos = cos_ref[...]
    sin = sin_ref[...]
    kn = ak * lax.rsqrt(_seg_mean_sq(ak, segk_ref[...]) + EPS) * kg_ref[...]
    k_scr[P:P + tile, :] = _rope(kn, cos, sin)
    v_scr[P:P + tile, :] = av
    yield
    ms = _seg_sum_sq(aq, segred_ref[...]) * (1.0 / A_HEAD_DIM)
    yield
    rs = lax.rsqrt(ms + EPS)
    lane = lax.broadcasted_iota(jnp.int32, rs.shape, 1)
    rs_wide = jnp.concatenate(
        [jnp.take_along_axis(rs, 2 * blk + (lane >= A_HEAD_DIM).astype(jnp.int32), axis=1)
         for blk in range(aq.shape[1] // LANES)], axis=1)
    qn = aq * rs_wide * qg_ref[...]
    qr = (_rope(qn, cos, sin) * (A_HEAD_DIM ** -0.5)).astype(BF16)
    yield

    lane_kv = lax.broadcasted_iota(jnp.int32, (P + qc, kvw), 1)
    lo_half = lane_kv < A_HEAD_DIM
    key_lane = lax.broadcasted_iota(jnp.int32, (1, 2 * P), 1)
    row4 = lax.broadcasted_iota(jnp.int32, (groups_pairs * qc, 1), 0)
    out_lane = lax.broadcasted_iota(jnp.int32, (1, kvw), 1)
    zpad = jnp.zeros((2 * P - (P + qc), kvw), BF16)

    blocks = [(c * qc, min(max(valid - c * qc, 0), qc)) for c in range(tile // qc)]

    sink_cols = {}
    for kvh in range(A_KV_HEADS):
        for half in range(2):
            col = jnp.zeros((groups_pairs * qc, 1), F32)
            for j in range(groups_pairs):
                col = jnp.where((row4 >= j * qc) & (row4 < (j + 1) * qc),
                                sinks_ref[(kvh * groups_pairs + j) * 2 + half], col)
            sink_cols[kvh, half] = col

    scores = {}
    for r0, own in blocks:
        if own == 0:
            o_scr[r0:r0 + qc, :] = jnp.zeros((qc, o_scr.shape[1]), F32)
            continue
        k_win = k_scr[r0:r0 + P + qc, :]
        k_rot = pltpu.roll(k_win, A_HEAD_DIM, axis=1)
        for kvh in range(A_KV_HEADS):
            k_a, k_b = (k_win, k_rot) if kvh == 0 else (k_rot, k_win)
            k_bd = jnp.concatenate([jnp.where(lo_half, k_a, 0.0).astype(BF16), zpad,
                                    jnp.where(lo_half, 0.0, k_b).astype(BF16), zpad], axis=0)
            pair0 = kvh * groups_pairs
            q_st = jnp.concatenate([qr[r0:r0 + qc, (pair0 + j) * kvw:(pair0 + j + 1) * kvw]
                                    for j in range(groups_pairs)], axis=0)
            scores[r0, kvh] = _mm_nt(q_st, k_bd)
            yield

        if has_history:
            first_valid = 0
        else:
            first_valid = P - jnp.minimum(t * tile + r0, P)
        key_ok = (key_lane >= first_valid) & (key_lane < P + own)
        v_win = v_scr[r0:r0 + P + qc, :]
        v_rot = pltpu.roll(v_win, A_HEAD_DIM, axis=1)
        for kvh in range(A_KV_HEADS):
            v_a, v_b = (v_win, v_rot) if kvh == 0 else (v_rot, v_win)
            v_bd = jnp.concatenate([jnp.where(lo_half, v_a, 0.0).astype(BF16), zpad,
                                    jnp.where(lo_half, 0.0, v_b).astype(BF16), zpad], axis=0)
            pair0 = kvh * groups_pairs
            s = scores[r0, kvh]
            probs, inv = [], []
            for half in range(2):
                s_h = jnp.where(key_ok, s[:, half * 2 * P:(half + 1) * 2 * P], -jnp.inf)
                sink = sink_cols[kvh, half]
                mx =jnp.maximum(jnp.max(s_h, axis=1, keepdims=True), sink)
                p = jnp.exp(s_h - mx)
                probs.append(p.astype(BF16))
                inv.append(1.0 / (jnp.sum(p, axis=1, keepdims=True) + jnp.exp(sink - mx)))
            o = jnp.dot(jnp.concatenate(probs, axis=1), v_bd, preferred_element_type=F32)
            o = o * jnp.where(out_lane < A_HEAD_DIM, inv[0], inv[1])
            for j in range(groups_pairs):
                o_scr[r0:r0 + qc, (pair0 + j) * kvw:(pair0 + j + 1) * kvw] = o[j * qc:(j + 1) * qc, :]
            yield

    y = _mm(o_scr[...], woc_ref[...])

    k_new = k_scr[valid:valid + P, :]
    v_new = v_scr[valid:valid + P, :]
    k_scr[0:P, :] = k_new
    v_scr[0:P, :] = v_new
    kn_ref[0] = k_new
    vn_ref[0] = v_new
    return jax.nn.sigmoid(_merge_pre(u, wgg_ref, merge_pre, "A") + bg_ref[...]) * y


def _merge_pre(u, wgg_ref, merge_pre, key):
    if key not in merge_pre:
        merge_pre[key] = _mm(u, wgg_ref[...])
    return merge_pre[key]


def _merge_gate_section(u, wgg_refs, merge_pre):
    for key, wgg_ref in wgg_refs.items():
        _merge_pre(u, wgg_ref, merge_pre, key)
        yield


N_LRU_W, N_MLSTM_W, N_ATTN_W = 11, 10, 11
ATTN_COS, ATTN_SIN = 7, 8
N_STATE = 7


PIECE_ORDER = (
    "LMAAM"
    "AGLM"
    "AGMM"
    "ALG"
    "MMMMM"
    "AAL"
    "MMAAAAMMAAAA"
    "MAAAALAAA"
)


def _trace_in_order(sections, order):
    results = {}

    def step(key):
        try:
            next(sections[key])
        except StopIteration as done:
            results[key] = done.value

    for key in order:
        if key not in results:
            step(key)
    while len(results) < len(sections):
        for key in sections:
            if key not in results:
                step(key)
    return results


def _mixer_kernel(*refs, tile, chunk, valid, has_history):
    it = iter(refs)
    take = lambda n: [next(it) for _ in range(n)]
    sinks_ref, x_ref, g1_ref = take(3)
    lru_w, mlstm_w, attn_w = take(N_LRU_W), take(N_MLSTM_W), take(N_ATTN_W)
    (wout_ref,) = take(1)
    st_in = take(N_STATE) if has_history else [None] * N_STATE
    (y_ref,) = take(1)
    st_out = take(N_STATE)
    xpad_scr, h_scr, lh_scr, c_scr, n_scr, m_scr, q_scr, k_scr, v_scr, hg_scr, ak_scr, av_scr, o_scr = take(13)

    t = pl.program_id(1)
    nt = pl.num_programs(1)

    @pl.when(t == 0)
    def _():
        _lru_init(st_in[0], st_in[1], xpad_scr, h_scr)
        _mlstm_init(st_in[2], st_in[3], st_in[4], c_scr, n_scr, m_scr)
        _attn_init(st_in[5], st_in[6], ak_scr, av_scr)

    x = x_ref[0]
    u = _rms(x, g1_ref[...]).astype(BF16)
    merge_pre = {}
    y = _trace_in_order({
        "L": _lru_section(u, *lru_w, st_out[0], st_out[1], xpad_scr, h_scr, lh_scr, merge_pre,
                          tile=tile, valid=valid),
        "M": _mlstm_section(u, *mlstm_w, c_scr, n_scr, m_scr, q_scr, k_scr, v_scr, hg_scr, merge_pre,
                            tile=tile, chunk=chunk, valid=valid),
        "A": _attn_section(u, t, sinks_ref, *attn_w, st_out[5], st_out[6], ak_scr, av_scr, o_scr, merge_pre,
                           tile=tile, valid=valid, has_history=has_history),
        "G": _merge_gate_section(u, {"L": lru_w[2], "M": mlstm_w[7], "A": attn_w[2]}, merge_pre)}, PIECE_ORDER)
    y_ref[0] = x + _mm((y["L"] + y["M"]) + y["A"], wout_ref[...])

    @pl.when(t == nt - 1)
    def _():
        st_out[2][0] = c_scr[...]
        st_out[3][0] = n_scr[...]
        st_out[4][0] = m_scr[...]


def _mixer_call(sinks, x, g1, lru_w, mlstm_w, attn_w, wout, state, *, tile, valid):
    B, S, D = x.shape
    W = lru_w[0].shape[1]
    MW = mlstm_w[0].shape[1]
    AW = attn_w[0].shape[1]
    KW = A_KV_HEADS * A_HEAD_DIM
    hd = MW // M_HEADS
    nt = S // tile
    chunk = min(MLSTM_CHUNK, tile)
    has_history = state is not None
    assert nt * tile == S and tile % chunk == 0 and tile % ATT_CHUNK == 0 and (valid == tile or nt == 1)
    assert len(lru_w) == N_LRU_W and len(mlstm_w) == N_MLSTM_W and len(attn_w) == N_ATTN_W

    def const(a):
        return pl.BlockSpec(a.shape, lambda b, t, _n=a.ndim: (0,) * _n, pipeline_mode=pl.Buffered(1))

    def per_batch(shape):
        return pl.BlockSpec((1,) + shape[1:], lambda b, t, _n=len(shape): (b,) + (0,) * (_n - 1))

    cos, sin = attn_w[ATTN_COS], attn_w[ATTN_SIN]
    attn_specs = [const(a) for a in attn_w]
    attn_specs[ATTN_COS] = pl.BlockSpec((tile, KW), lambda b, t: (t, 0))
    attn_specs[ATTN_SIN] = pl.BlockSpec((tile, KW), lambda b, t: (t, 0))
    assert cos.shape == (S, KW) and sin.shape == (S, KW)

    state_shapes = [(B, CONV_W - 1, W), (B, 1, W), (B, M_HEADS, hd, hd), (B, M_HEADS, hd), (B, 1, LANES),
                    (B, WINDOW, KW), (B, WINDOW, KW)]
    in_specs = ([pl.BlockSpec(memory_space=pltpu.SMEM),
                 pl.BlockSpec((1, tile, D), lambda b, t: (b, t, 0)), const(g1)]
                + [const(a) for a in lru_w] + [const(a) for a in mlstm_w] + attn_specs + [const(wout)])
    args = [sinks, x, g1, *lru_w, *mlstm_w, *attn_w, wout]
    if has_history:
        in_specs += [per_batch(s) for s in state_shapes]
        args += list(state)
    kern = functools.partial(_mixer_kernel, tile=tile, chunk=chunk, valid=valid, has_history=has_history)
    return pl.pallas_call(
        kern,
        grid=(B, nt),
        in_specs=in_specs,
        out_specs=[pl.BlockSpec((1, tile, D), lambda b, t: (b, t, 0))] + [per_batch(s) for s in state_shapes],
        out_shape=[jax.ShapeDtypeStruct((B, S, D), F32)] + [jax.ShapeDtypeStruct(s, F32) for s in state_shapes],
        scratch_shapes=[
            pltpu.VMEM((tile + SUBLANES, W), F32), pltpu.VMEM((1, W), F32), pltpu.VMEM((tile, W), F32),
            pltpu.VMEM((M_HEADS, hd, hd), F32), pltpu.VMEM((M_HEADS, hd), F32), pltpu.VMEM((1, LANES), F32),
            pltpu.VMEM((tile, MW), F32), pltpu.VMEM((tile, MW), F32), pltpu.VMEM((tile, MW), F32),
            pltpu.VMEM((tile, MW), F32),
            pltpu.VMEM((WINDOW + tile, KW), F32), pltpu.VMEM((WINDOW + tile, KW), F32),
            pltpu.VMEM((tile, AW), F32),
        ],
        compiler_params=pltpu.CompilerParams(dimension_semantics=("arbitrary", "arbitrary"),
                                             vmem_limit_bytes=VMEM_LIMIT),
        name="mixer",
    )(*args)


def _mlp_kernel(x_ref, g2_ref, wup_ref, wdown_ref, o_ref):
    x = x_ref[...]
    up = _mm(_rms(x, g2_ref[...]), wup_ref[...])
    o_ref[...] = x + _mm(jnp.square(jax.nn.relu(up)), wdown_ref[...])


def _mlp_call(x, g2, wup, wdown, *, tile):
    N, D = x.shape
    FF = wup.shape[1]
    assert N % tile == 0
    row = lambda i: (i, 0)
    const = lambda shape: pl.BlockSpec(shape, lambda i: (0, 0), pipeline_mode=pl.Buffered(1))
    return pl.pallas_call(
        _mlp_kernel,
        grid=(N // tile,),
        in_specs=[pl.BlockSpec((tile, D), row), const((1, D)), const((D, FF)), const((FF, D))],
        out_specs=pl.BlockSpec((tile, D), row),
        out_shape=jax.ShapeDtypeStruct((N, D), F32),
        compiler_params=pltpu.CompilerParams(dimension_semantics=("arbitrary",),
                                             vmem_limit_bytes=VMEM_LIMIT),
        name="mlp",
    )(x, g2, wup, wdown)


def _rope_tables(pos, lanes):
    half = A_HEAD_DIM // 2
    inv = ROPE_THETA ** (-jnp.arange(half, dtype=F32) / half)
    ang = pos.astype(F32)[:, None] * inv[None, :]
    cos, sin = jnp.cos(ang), jnp.sin(ang)
    reps = lanes // A_HEAD_DIM
    return (jnp.tile(jnp.concatenate([cos, cos], axis=1), (1, reps)),
            jnp.tile(jnp.concatenate([-sin, sin], axis=1), (1, reps)))


def _seg_ones(width):
    seg = jnp.arange(width) // A_HEAD_DIM
    return (seg[:, None] == seg[None, :]).astype(BF16)


def _seg_select(width):
    seg = jnp.arange(width) // A_HEAD_DIM
    return (seg[:, None] == jnp.arange(LANES)[None, :]).astype(BF16)


def _paired_gate_weights(wa, wx):
    nb, blk, _ = wa.shape
    per = MXU_DIM // blk
    eye = jnp.eye(per, dtype=wa.dtype)[None, :, None, :, None]

    def block_diag(w):
        return (w.reshape(nb // per, per, blk, 1, blk) * eye).reshape(nb // per, per * blk, per * blk)

    return jnp.concatenate([block_diag(wa), block_diag(wx)], axis=-1).astype(BF16)


def _layer_weights(w):
    (norm1_g, w_in, conv_w, conv_b, lru_wa, lru_ba, lru_wx, lru_bx, lru_lam, m_bi, m_bf, m_norm_g, qn_g, kn_g,
     sinks, w_oa, w_ob, w_oc, b_gate, w_out, norm2_g, w_up, w_down) = w
    D = w_in.shape[0]
    W = conv_w.shape[1]
    MW = m_norm_g.shape[0]
    AW = w_oc.shape[0]
    KW = A_KV_HEADS * A_HEAD_DIM
    sizes = (W, W, MW, MW, MW, MW, M_HEADS, M_HEADS, AW, KW, KW, D, D, D)
    offs = [0]
    for s_ in sizes:
        offs.append(offs[-1] + s_)
    col = lambda i: w_in[:, offs[i]:offs[i + 1]].astype(BF16)
    row = lambda v: v.reshape(1, -1)

    lru_w = [col(0), col(1), col(11), conv_w, row(conv_b), _paired_gate_weights(lru_wa, lru_wx), row(lru_ba),
             row(lru_bx), row(lru_lam), row(b_gate[0]), w_oa.astype(BF16)]
    wif = jnp.zeros((D, LANES), BF16).at[:, :2 * M_HEADS].set(w_in[:, offs[6]:offs[8]].astype(BF16))
    bif = jnp.zeros((1, LANES), F32).at[0, :2 * M_HEADS].set(jnp.concatenate([m_bi, m_bf]))
    mlstm_w = [col(2), col(3), col(4), col(5), wif, bif, row(m_norm_g), col(12), row(b_gate[1]),
               w_ob.astype(BF16)]
    attn_w = [col(8), w_in[:, offs[9]:offs[11]].astype(BF16), col(13), row(jnp.tile(qn_g, AW // A_HEAD_DIM)),
              row(jnp.tile(kn_g, A_KV_HEADS)), _seg_select(AW), _seg_ones(KW), None, None, row(b_gate[2]),
              w_oc.astype(BF16)]
    return (sinks, row(norm1_g), lru_w, mlstm_w, attn_w, w_out.astype(BF16),
            row(norm2_g), w_up.astype(BF16), w_down.astype(BF16))


def _layer(x, state, lw, cos, sin, *, tile, valid):
    sinks, g1, lru_w, mlstm_w, attn_w, wout, g2, wup, wdown = lw
    B, S, D = x.shape
    attn_w = attn_w[:ATTN_COS] + [cos, sin] + attn_w[ATTN_SIN + 1:]
    if S < tile:
        x = jnp.pad(x, ((0, 0), (0, tile - S), (0, 0)))
    outs = _mixer_call(sinks, x, g1, lru_w, mlstm_w, attn_w, wout, state, tile=tile, valid=valid)
    x1, new_state = outs[0][:, :S], outs[1:]
    N = B * S
    x2 = _mlp_call(x1.reshape(N, D), g2, wup, wdown, tile=min(MLP_TILE, N)).reshape(B, S, D)
    return x2, new_state


def kernel(x_prompt, x_sample, state_conv, state_lru, state_mlstm_C, state_mlstm_n, state_mlstm_m, cache_k, cache_v, norm1_g, w_in, conv_w, conv_b, lru_wa, lru_ba, lru_wx, lru_bx, lru_lam, m_bi, m_bf, m_norm_g, qn_g, kn_g, sinks, w_oa, w_ob, w_oc, b_gate, w_out, norm2_g, w_up, w_down):
    B, S, D = x_prompt.shape
    Bs, Sd, _ = x_sample.shape
    depth = w_in.shape[0]
    W = conv_w.shape[-1]
    KW = A_KV_HEADS * A_HEAD_DIM
    p_tile = min(PROMPT_TILE, S)
    assert S % p_tile == 0 and Sd <= SAMPLE_TILE and cache_k.shape[2] == WINDOW

    cos_p, sin_p = _rope_tables(jnp.arange(S, dtype=jnp.int32), KW)
    cos_s, sin_s = _rope_tables(PAST_LEN + jnp.arange(SAMPLE_TILE, dtype=jnp.int32), KW)

    y_p = x_prompt
    y_s = x_sample
    new_p, new_s = [], []
    for l in range(depth):
        lw = _layer_weights((norm1_g[l], w_in[l], conv_w[l], conv_b[l], lru_wa[l], lru_ba[l], lru_wx[l], lru_bx[l],
                             lru_lam[l], m_bi[l], m_bf[l], m_norm_g[l], qn_g[l], kn_g[l], sinks[l], w_oa[l], w_ob[l],
                             w_oc[l], b_gate[l], w_out[l], norm2_g[l], w_up[l], w_down[l]))
        y_p, sp = _layer(y_p, None, lw, cos_p, sin_p, tile=p_tile, valid=p_tile)
        m0 = jnp.zeros((Bs, 1, LANES), F32).at[:, 0, :M_HEADS].set(state_mlstm_m[l])
        st = (state_conv[l], state_lru[l].reshape(Bs, 1, W), state_mlstm_C[l], state_mlstm_n[l], m0,
              cache_k[l].reshape(Bs, WINDOW, KW), cache_v[l].reshape(Bs, WINDOW, KW))
        y_s, ss = _layer(y_s, st, lw, cos_s, sin_s, tile=SAMPLE_TILE, valid=Sd)
        new_p.append(sp)
        new_s.append(ss)

    def unpack(states):
        conv, h, c, n, m, k, v = [jnp.stack(s) for s in zip(*states)]
        nb = conv.shape[1]
        return (conv, h.reshape(depth, nb, W), c, n, m[:, :, 0, :M_HEADS],
                k.reshape(depth, nb, WINDOW, A_KV_HEADS, A_HEAD_DIM),
                v.reshape(depth, nb, WINDOW, A_KV_HEADS, A_HEAD_DIM))

    return (y_p, y_s, *unpack(new_p), *unpack(new_s))
```

```python
import functools

import jax
import jax.numpy as jnp
from jax import lax
from jax.experimental import pallas as pl
from jax.experimental.pallas import tpu as pltpu

F32 = jnp.float32
BF16 = jnp.bfloat16

EPS = 1e-6
CONV_W = 4
LRU_BLOCKS = 8
LRU_C = 8.0
M_HEADS = 4
A_HEAD_DIM = 64
A_KV_HEADS = 2
ATT_CHUNK = 64
WINDOW = 128
PAST_LEN = 2048
ROPE_THETA = 10000.0

SUBLANES = 8
LANES = 128
MXU_DIM = 256
MLSTM_CHUNK = 256
LRU_ROWS = 32
PROMPT_TILE = 256
SAMPLE_TILE = 128
MLP_TILE = 512
VMEM_LIMIT = 60 * 1024 * 1024


def _mm(a, b):
    return jnp.dot(a.astype(BF16), b.astype(BF16), preferred_element_type=F32)


def _mm_nt(a, b):
    return lax.dot_general(a.astype(BF16), b.astype(BF16), (((1,), (1,)), ((), ())),
                           preferred_element_type=F32)


def _mm_tn(a, b):
    return lax.dot_general(a.astype(BF16), b.astype(BF16), (((0,), (0,)), ((), ())),
                           preferred_element_type=F32)


def _mm_split3(a, b01):
    hi = a.astype(BF16)
    rest = a - hi.astype(F32)
    mid = rest.astype(BF16)
    lo = (rest - mid.astype(F32)).astype(BF16)
    dot = lambda p: jnp.dot(p, b01, preferred_element_type=F32)
    return (dot(hi) + dot(mid)) + dot(lo)


def _cumsum_lanes(x, triu01):
    return _mm_split3(x, triu01)


def _rms(x, g):
    ms = jnp.mean(x * x, axis=-1, keepdims=True)
    return (x * lax.rsqrt(ms + EPS)) * g


def _linear_scan(a, b, h0):
    rows, width = a.shape
    sub = lax.broadcasted_iota(jnp.int32, (SUBLANES, width), 0)
    out = []
    carry = h0
    for g in range(rows // SUBLANES):
        a_g = a[g * SUBLANES:(g + 1) * SUBLANES, :]
        b_g = b[g * SUBLANES:(g + 1) * SUBLANES, :]
        d = 1
        while d < SUBLANES:
            m = sub >= d
            b_g = jnp.where(m, a_g * pltpu.roll(b_g, d, axis=0) + b_g, b_g)
            a_g = jnp.where(m, a_g * pltpu.roll(a_g, d, axis=0), a_g)
            d *= 2
        h_g = b_g + a_g * carry
        out.append(h_g)
        carry = h_g[SUBLANES - 1:SUBLANES, :]
    return jnp.concatenate(out, axis=0)


def _lru_init(conv0_ref, h0_ref, xpad_scr, h_scr):
    pad = SUBLANES
    if conv0_ref is None:
        xpad_scr[0:pad, :] = jnp.zeros((pad, xpad_scr.shape[1]), F32)
        h_scr[...] = jnp.zeros(h_scr.shape, F32)
    else:
        xpad_scr[pad - (CONV_W - 1):pad, :] = conv0_ref[0]
        h_scr[...] = h0_ref[0]


def _lru_section(u, wxa_ref, wga_ref, wgg_ref, convw_ref, convb_ref, wgate_ref, ba_ref, bx_ref, lam_ref,
                 bg_ref, woa_ref, convn_ref, hlast_ref, xpad_scr, h_scr, hg_scr, merge_pre, *, tile, valid):
    pad = SUBLANES
    ntail = CONV_W - 1

    xa = _mm(u, wxa_ref[...])
    xpad_scr[pad:pad + tile, :] = xa
    prev = [xpad_scr[pad - k:pad - k + 1, :] for k in range(1, CONV_W)]
    row = lax.broadcasted_iota(jnp.int32, (SUBLANES, xa.shape[1]), 0)

    def later(z, first):
        rolled = pltpu.roll(z, 1, axis=0)
        head = jnp.where(row == 0, first, rolled[0:SUBLANES, :])
        return jnp.concatenate([head, rolled[SUBLANES:, :]], axis=0)

    acc, acc_prev = None, None
    for j in range(CONV_W):
        w_j = convw_ref[j:j + 1, :]
        term = xa * w_j
        if acc is None:
            acc, acc_prev = term, prev[0] * w_j
        else:
            acc = term + later(acc, acc_prev)
            acc_prev = sum(prev[k] * convw_ref[j - k:j - k + 1, :] for k in range(j + 1)) if j < CONV_W - 1 else None
    xc = acc + convb_ref[...]
    yield

    xcb = xc.astype(BF16)
    pairs = wgate_ref.shape[0]
    gates = [jnp.dot(xcb[:, p * MXU_DIM:(p + 1) * MXU_DIM], wgate_ref[p], preferred_element_type=F32)
             for p in range(pairs)]
    r_pre = jnp.concatenate([g[:, :MXU_DIM] for g in gates], axis=1) + ba_ref[...]
    i_pre = jnp.concatenate([g[:, MXU_DIM:] for g in gates], axis=1) + bx_ref[...]
    yield
    ga = _mm(u, wga_ref[...])
    yield

    carry = h_scr[...]
    sp = -LRU_C * jax.nn.softplus(-lam_ref[...])
    for r0 in range(0, tile, LRU_ROWS):
        rows = slice(r0, r0 + LRU_ROWS)
        if r0 >= valid:
            hg_scr[rows, :] = jnp.zeros((LRU_ROWS, hg_scr.shape[1]), F32)
            continue
        r = jax.nn.sigmoid(r_pre[rows])
        i = jax.nn.sigmoid(i_pre[rows])
        log_a = r * sp
        a = jnp.exp(log_a)
        th = jnp.tanh(log_a)
        uu = jnp.sqrt((-2.0 * th) / (1.0 - th)) * (i * xc[rows])
        h = _linear_scan(a, uu, carry)
        carry = h[LRU_ROWS - 1:LRU_ROWS, :]
        if r0 < valid <= r0 + LRU_ROWS:
            h_last = h[valid - 1 - r0:valid - r0, :]
            h_scr[...] = h_last
            hlast_ref[0] = h_last
        hg_scr[rows, :] = h * jax.nn.gelu(ga[rows])
    yield

    y = _mm(hg_scr[...], woa_ref[...])
    tail = xpad_scr[pad + valid - ntail:pad + valid, :]
    xpad_scr[pad - ntail:pad, :] = tail
    convn_ref[0] = tail
    return jax.nn.sigmoid(_merge_pre(u, wgg_ref, merge_pre, "L") + bg_ref[...]) * y


def _mlstm_init(c0_ref, n0_ref, m0_ref, c_scr, n_scr, m_scr):
    if c0_ref is None:
        c_scr[...] = jnp.zeros(c_scr.shape, F32)
        n_scr[...] = jnp.zeros(n_scr.shape, F32)
        m_scr[...] = jnp.zeros(m_scr.shape, F32)
    else:
        c_scr[...] = c0_ref[0]
        n_scr[...] = n0_ref[0]
        m_scr[...] = m0_ref[0]


def _mlstm_section(u, wq_ref, wk_ref, wv_ref, wo_ref, wif_ref, bif_ref, mg_ref, wgg_ref, bg_ref, wob_ref,
                   c_scr, n_scr, m_scr, q_scr, k_scr, v_scr, hg_scr, merge_pre, *, tile, chunk, valid):
    hd = q_scr.shape[1] // M_HEADS
    blocks = [(c * chunk, min(max(valid - c * chunk, 0), chunk)) for c in range(tile // chunk)]

    gates = _mm(u, wif_ref[...]) + bif_ref[...]
    glane = lax.broadcasted_iota(jnp.int32, gates.shape, 1)
    gates = jnp.where((glane >= M_HEADS) & (glane < 2 * M_HEADS), jax.nn.log_sigmoid(gates), gates)
    yield
    q_scr[...] = _mm(u, wq_ref[...])
    yield
    k_scr[...] = _mm(u, wk_ref[...]) * (hd ** -0.5)
    yield
    v_scr[...] = _mm(u, wv_ref[...])
    yield

    rr = lax.broadcasted_iota(jnp.int32, (chunk, chunk), 0)
    cc = lax.broadcasted_iota(jnp.int32, (chunk, chunk), 1)
    causal = cc <= rr
    triu = (rr <= cc).astype(BF16)
    grow = 2 * SUBLANES
    mlane = lax.broadcasted_iota(jnp.int32, (1, LANES), 1)
    ccol = lax.broadcasted_iota(jnp.int32, (1, chunk), 1)
    crow = lax.broadcasted_iota(jnp.int32, (chunk, 1), 0)

    cums = []
    for r0, vl in blocks:
        g_c = gates[r0:r0 + chunk, :]
        g_rows = g_c.T[0:grow, :]
        b_rows = _cumsum_lanes(g_rows, triu)
        b_all = jnp.concatenate([b_rows, jnp.zeros((LANES - grow, chunk), F32)], axis=0).T
        cums.append((g_c, g_rows, b_rows, b_all))
    yield
    og = None

    for (r0, vl), (g_c, g_rows, b_rows, b_all) in zip(blocks, cums):
        if vl == 0:
            hg_scr[r0:r0 + chunk, :] = jnp.zeros((chunk, hg_scr.shape[1]), F32)
            continue
        m_row = m_scr[...]
        heads = []
        for h in range(M_HEADS):
            cs = slice(h * hd, (h + 1) * hd)
            q_h = q_scr[r0:r0 + chunk, cs]
            k_h = k_scr[r0:r0 + chunk, cs]
            b_col = b_all[:, M_HEADS + h:M_HEADS + h + 1]
            b_row = b_rows[M_HEADS + h:M_HEADS + h + 1, :]
            ig_row = g_rows[h:h + 1, :]
            m_prev = m_row[:, h:h + 1]
            log_d = jnp.where(causal, (b_col - b_row) + ig_row, -jnp.inf)
            inter = b_col + m_prev
            m_t = jnp.maximum(inter, jnp.max(log_d, axis=1, keepdims=True))
            s = _mm_nt(q_h, k_h)
            qc = _mm(q_h, c_scr[h])
            heads.append((q_h, k_h, b_col, b_row, ig_row, m_prev, log_d, inter, m_t, s, qc))
            yield
        if og is None:
            og = jax.nn.sigmoid(_mm(u, wo_ref[...]))
            yield
        for h in range(M_HEADS):
            cs = slice(h * hd, (h + 1) * hd)
            q_h, k_h, b_col, b_row, ig_row, m_prev, log_d, inter, m_t, s, qc = heads[h]
            v_h = v_scr[r0:r0 + chunk, cs]
            ig_col = g_c[:, h:h + 1]
            c_h = c_scr[h]
            n_h = n_scr[h:h + 1, :]
            w = s * jnp.exp(log_d - m_t)
            inter_w = jnp.exp(inter - m_t)
            num = _mm(w, v_h) + inter_w * qc
            den = jnp.sum(w, axis=1, keepdims=True) + inter_w * jnp.sum(q_h * n_h, axis=1, keepdims=True)
            hh = num / jnp.maximum(jnp.abs(den), jnp.exp(-m_t))

            b_last = b_col[vl - 1:vl, :]
            tail_row = (b_last - b_row) + ig_row
            tail_col = (b_last - b_col) + ig_col
            if vl < chunk:
                tail_row = jnp.where(ccol < vl, tail_row, -jnp.inf)
                tail_col = jnp.where(crow < vl, tail_col, -jnp.inf)
            m_new = jnp.maximum(b_last + m_prev, jnp.max(tail_row, axis=1, keepdims=True))
            decay = jnp.exp((b_last + m_prev) - m_new)
            kw = k_h * jnp.exp(tail_col - m_new)
            c_scr[h] = decay * c_h + _mm_tn(kw, v_h)
            n_scr[h:h + 1, :] = decay * n_h + jnp.sum(kw, axis=0, keepdims=True)
            m_row = jnp.where(mlane == h, m_new, m_row)

            hn = _rms(hh, mg_ref[:, cs])
            hg_scr[r0:r0 + chunk, cs] = hn * og[r0:r0 + chunk, cs]
            if h == M_HEADS - 1:
                m_scr[...] = m_row
            yield

    y = _mm(hg_scr[...], wob_ref[...])
    return jax.nn.sigmoid(_merge_pre(u, wgg_ref, merge_pre, "M") + bg_ref[...]) * y


def _seg_sum_sq(x, seg01):
    sq = x * x
    hi = sq.astype(BF16)
    lo = (sq - hi.astype(F32)).astype(BF16)
    return jnp.dot(hi, seg01, preferred_element_type=F32) + jnp.dot(lo, seg01, preferred_element_type=F32)


def _seg_mean_sq(x, seg_ones):
    return _seg_sum_sq(x, seg_ones) * (1.0 / A_HEAD_DIM)


def _rope(x, cos, sin_signed):
    width = x.shape[1]
    half = A_HEAD_DIM // 2
    lane = lax.broadcasted_iota(jnp.int32, x.shape, 1)
    partner = jnp.where((lane % A_HEAD_DIM) < half,
                        pltpu.roll(x, width - half, axis=1),
                        pltpu.roll(x, half, axis=1))
    reps = width // cos.shape[1]
    if reps > 1:
        cos = jnp.concatenate([cos] * reps, axis=1)
        sin_signed = jnp.concatenate([sin_signed] * reps, axis=1)
    return x * cos + partner * sin_signed


def _attn_init(k0_ref, v0_ref, k_scr, v_scr):
    P, kvw = WINDOW, k_scr.shape[1]
    if k0_ref is None:
        k_scr[0:P, :] = jnp.zeros((P, kvw), F32)
        v_scr[0:P, :] = jnp.zeros((P, kvw), F32)
    else:
        k_scr[0:P, :] = k0_ref[0]
        v_scr[0:P, :] = v0_ref[0]


def _attn_section(u, t, sinks_ref, wq_ref, wkv_ref, wgg_ref, qg_ref, kg_ref, segred_ref, segk_ref,
                  cos_ref, sin_ref, bg_ref, woc_ref, kn_ref, vn_ref,
                  k_scr, v_scr, o_scr, merge_pre, *, tile, valid, has_history):
    P = WINDOW
    qc = ATT_CHUNK
    kvw = k_scr.shape[1]
    groups_pairs = (wq_ref.shape[1] // kvw) // A_KV_HEADS

    aq = _mm(u, wq_ref[...])
    yield
    akv = _mm(u, wkv_ref[...])
    ak, av = akv[:, :kvw], akv[:, kvw:]
    yield
    cos = cos_ref[...]
    sin = sin_ref[...]
    kn = ak * lax.rsqrt(_seg_mean_sq(ak, segk_ref[...]) + EPS) * kg_ref[...]
    k_scr[P:P + tile, :] = _rope(kn, cos, sin)
    v_scr[P:P + tile, :] = av
    yield
    ms = _seg_sum_sq(aq, segred_ref[...]) * (1.0 / A_HEAD_DIM)
    yield
    rs = lax.rsqrt(ms + EPS)
    lane = lax.broadcasted_iota(jnp.int32, rs.shape, 1)
    rs_wide = jnp.concatenate(
        [jnp.take_along_axis(rs, 2 * blk + (lane >= A_HEAD_DIM).astype(jnp.int32), axis=1)
         for blk in range(aq.shape[1] // LANES)], axis=1)
    qn = aq * rs_wide * (qg_ref[...] * (A_HEAD_DIM ** -0.5))
    qr = _rope(qn, cos, sin).astype(BF16)
    yield

    lane_kv = lax.broadcasted_iota(jnp.int32, (P + qc, kvw), 1)
    lo_half = lane_kv < A_HEAD_DIM
    key_lane = lax.broadcasted_iota(jnp.int32, (1, 2 * P), 1)
    row4 = lax.broadcasted_iota(jnp.int32, (groups_pairs * qc, 1), 0)
    out_lane = lax.broadcasted_iota(jnp.int32, (1, kvw), 1)
    zpad = jnp.zeros((2 * P - (P + qc), kvw), BF16)

    blocks = [(c * qc, min(max(valid - c * qc, 0), qc)) for c in range(tile // qc)]

    sink_cols = {}
    for kvh in range(A_KV_HEADS):
        for half in range(2):
            col = jnp.zeros((groups_pairs * qc, 1), F32)
            for j in range(groups_pairs):
                col = jnp.where((row4 >= j * qc) & (row4 < (j + 1) * qc),
                                sinks_ref[(kvh * groups_pairs + j) * 2 + half], col)
            sink_cols[kvh, half] = col

    scores = {}
    for r0, own in blocks:
        if own == 0:
            o_scr[r0:r0 + qc, :] = jnp.zeros((qc, o_scr.shape[1]), F32)
            continue
        k_win = k_scr[r0:r0 + P + qc, :]
        k_rot = pltpu.roll(k_win, A_HEAD_DIM, axis=1)
        for kvh in range(A_KV_HEADS):
            k_a, k_b = (k_win, k_rot) if kvh == 0 else (k_rot, k_win)
            k_bd = jnp.concatenate([jnp.where(lo_half, k_a, 0.0).astype(BF16), zpad,
                                    jnp.where(lo_half, 0.0, k_b).astype(BF16), zpad], axis=0)
            pair0 = kvh * groups_pairs
            q_st = jnp.concatenate([qr[r0:r0 + qc, (pair0 + j) * kvw:(pair0 + j + 1) * kvw]
                                    for j in range(groups_pairs)], axis=0)
            scores[r0, kvh] = _mm_nt(q_st, k_bd)
            yield

        if has_history:
            first_valid = 0
        else:
            first_valid = P - jnp.minimum(t * tile + r0, P)
        key_ok = (key_lane >= first_valid) & (key_lane < P + own)
        v_win = v_scr[r0:r0 + P + qc, :]
        v_rot = pltpu.roll(v_win, A_HEAD_DIM, axis=1)
        for kvh in range(A_KV_HEADS):
            v_a, v_b = (v_win, v_rot) if kvh == 0 else (v_rot, v_win)
            v_bd = jnp.concatenate([jnp.where(lo_half, v_a, 0.0).astype(BF16), zpad,
                                    jnp.where(lo_half, 0.0, v_b).astype(BF16), zpad], axis=0)
            pair0 = kvh * groups_pairs
            s = scores[r0, kvh]
            probs, inv = [], []
            for half in range(2):
                s_h = jnp.where(key_ok, s[:, half * 2 * P:(half + 1) * 2 * P], -jnp.inf)
                sink = sink_cols[kvh, half]
                mx =jnp.maximum(jnp.max(s_h, axis=1, keepdims=True), sink)
                p = jnp.exp(s_h - mx)
                probs.append(p.astype(BF16))
                inv.append(1.0 / (jnp.sum(p, axis=1, keepdims=True) + jnp.exp(sink - mx)))
            o = jnp.dot(jnp.concatenate(probs, axis=1), v_bd, preferred_element_type=F32)
            o = o * jnp.where(out_lane < A_HEAD_DIM, inv[0], inv[1])
            for j in range(groups_pairs):
                o_scr[r0:r0 + qc, (pair0 + j) * kvw:(pair0 + j + 1) * kvw] = o[j * qc:(j + 1) * qc, :]
            yield

    y = _mm(o_scr[...], woc_ref[...])

    k_new = k_scr[valid:valid + P, :]
    v_new = v_scr[valid:valid + P, :]
    k_scr[0:P, :] = k_new
    v_scr[0:P, :] = v_new
    kn_ref[0] = k_new
    vn_ref[0] = v_new
    return jax.nn.sigmoid(_merge_pre(u, wgg_ref, merge_pre, "A") + bg_ref[...]) * y


def _merge_pre(u, wgg_ref, merge_pre, key):
    if key not in merge_pre:
        merge_pre[key] = _mm(u, wgg_ref[...])
    return merge_pre[key]


def _merge_gate_section(u, wgg_refs, merge_pre):
    for key, wgg_ref in wgg_refs.items():
        _merge_pre(u, wgg_ref, merge_pre, key)
        yield


N_LRU_W, N_MLSTM_W, N_ATTN_W = 11, 10, 11
ATTN_COS, ATTN_SIN = 7, 8
N_STATE = 7


PIECE_ORDER = (
    "LMAAM"
    "AGLM"
    "AGMM"
    "ALG"
    "MMMMM"
    "AAL"
    "MMAAAAMMAAAA"
    "MAAAALAAA"
)


def _trace_in_order(sections, order):
    results = {}

    def step(key):
        try:
            next(sections[key])
        except StopIteration as done:
            results[key] = done.value

    for key in order:
        if key not in results:
            step(key)
    while len(results) < len(sections):
        for key in sections:
            if key not in results:
                step(key)
    return results


def _mixer_kernel(*refs, tile, chunk, valid, has_history):
    it = iter(refs)
    take = lambda n: [next(it) for _ in range(n)]
    sinks_ref, x_ref, g1_ref = take(3)
    lru_w, mlstm_w, attn_w = take(N_LRU_W), take(N_MLSTM_W), take(N_ATTN_W)
    (wout_ref,) = take(1)
    st_in = take(N_STATE) if has_history else [None] * N_STATE
    (y_ref,) = take(1)
    st_out = take(N_STATE)
    xpad_scr, h_scr, lh_scr, c_scr, n_scr, m_scr, q_scr, k_scr, v_scr, hg_scr, ak_scr, av_scr, o_scr = take(13)

    t = pl.program_id(1)
    nt = pl.num_programs(1)

    @pl.when(t == 0)
    def _():
        _lru_init(st_in[0], st_in[1], xpad_scr, h_scr)
        _mlstm_init(st_in[2], st_in[3], st_in[4], c_scr, n_scr, m_scr)
        _attn_init(st_in[5], st_in[6], ak_scr, av_scr)

    x = x_ref[0]
    u = _rms(x, g1_ref[...]).astype(BF16)
    merge_pre = {}
    y = _trace_in_order({
        "L": _lru_section(u, *lru_w, st_out[0], st_out[1], xpad_scr, h_scr, lh_scr, merge_pre,
                          tile=tile, valid=valid),
        "M": _mlstm_section(u, *mlstm_w, c_scr, n_scr, m_scr, q_scr, k_scr, v_scr, hg_scr, merge_pre,
                            tile=tile, chunk=chunk, valid=valid),
        "A": _attn_section(u, t, sinks_ref, *attn_w, st_out[5], st_out[6], ak_scr, av_scr, o_scr, merge_pre,
                           tile=tile, valid=valid, has_history=has_history),
        "G": _merge_gate_section(u, {"L": lru_w[2], "M": mlstm_w[7], "A": attn_w[2]}, merge_pre)}, PIECE_ORDER)
    y_ref[0] = x + _mm((y["L"] + y["M"]) + y["A"], wout_ref[...])

    @pl.when(t == nt - 1)
    def _():
        st_out[2][0] = c_scr[...]
        st_out[3][0] = n_scr[...]
        st_out[4][0] = m_scr[...]


def _mixer_call(sinks, x, g1, lru_w, mlstm_w, attn_w, wout, state, *, tile, valid):
    B, S, D = x.shape
    W = lru_w[0].shape[1]
    MW = mlstm_w[0].shape[1]
    AW = attn_w[0].shape[1]
    KW = A_KV_HEADS * A_HEAD_DIM
    hd = MW // M_HEADS
    nt = S // tile
    chunk = min(MLSTM_CHUNK, tile)
    has_history = state is not None
    assert nt * tile == S and tile % chunk == 0 and tile % ATT_CHUNK == 0 and (valid == tile or nt == 1)
    assert len(lru_w) == N_LRU_W and len(mlstm_w) == N_MLSTM_W and len(attn_w) == N_ATTN_W

    def const(a):
        return pl.BlockSpec(a.shape, lambda b, t, _n=a.ndim: (0,) * _n, pipeline_mode=pl.Buffered(1))

    def per_batch(shape):
        return pl.BlockSpec((1,) + shape[1:], lambda b, t, _n=len(shape): (b,) + (0,) * (_n - 1))

    cos, sin = attn_w[ATTN_COS], attn_w[ATTN_SIN]
    attn_specs = [const(a) for a in attn_w]
    attn_specs[ATTN_COS] = pl.BlockSpec((tile, KW), lambda b, t: (t, 0))
    attn_specs[ATTN_SIN] = pl.BlockSpec((tile, KW), lambda b, t: (t, 0))
    assert cos.shape == (S, KW) and sin.shape == (S, KW)

    state_shapes = [(B, CONV_W - 1, W), (B, 1, W), (B, M_HEADS, hd, hd), (B, M_HEADS, hd), (B, 1, LANES),
                    (B, WINDOW, KW), (B, WINDOW, KW)]
    in_specs = ([pl.BlockSpec(memory_space=pltpu.SMEM),
                 pl.BlockSpec((1, tile, D), lambda b, t: (b, t, 0)), const(g1)]
                + [const(a) for a in lru_w] + [const(a) for a in mlstm_w] + attn_specs + [const(wout)])
    args = [sinks, x, g1, *lru_w, *mlstm_w, *attn_w, wout]
    if has_history:
        in_specs += [per_batch(s) for s in state_shapes]
        args += list(state)
    kern = functools.partial(_mixer_kernel, tile=tile, chunk=chunk, valid=valid, has_history=has_history)
    return pl.pallas_call(
        kern,
        grid=(B, nt),
        in_specs=in_specs,
        out_specs=[pl.BlockSpec((1, tile, D), lambda b, t: (b, t, 0))] + [per_batch(s) for s in state_shapes],
        out_shape=[jax.ShapeDtypeStruct((B, S, D), F32)] + [jax.ShapeDtypeStruct(s, F32) for s in state_shapes],
        scratch_shapes=[
            pltpu.VMEM((tile + SUBLANES, W), F32), pltpu.VMEM((1, W), F32), pltpu.VMEM((tile, W), F32),
            pltpu.VMEM((M_HEADS, hd, hd), F32), pltpu.VMEM((M_HEADS, hd), F32), pltpu.VMEM((1, LANES), F32),
            pltpu.VMEM((tile, MW), F32), pltpu.VMEM((tile, MW), F32), pltpu.VMEM((tile, MW), F32),
            pltpu.VMEM((tile, MW), F32),
            pltpu.VMEM((WINDOW + tile, KW), F32), pltpu.VMEM((WINDOW + tile, KW), F32),
            pltpu.VMEM((tile, AW), F32),
        ],
        compiler_params=pltpu.CompilerParams(dimension_semantics=("arbitrary", "arbitrary"),
                                             vmem_limit_bytes=VMEM_LIMIT),
        name="mixer",
    )(*args)


def _mlp_kernel(x_ref, g2_ref, wup_ref, wdown_ref, o_ref):
    x = x_ref[...]
    up = _mm(_rms(x, g2_ref[...]), wup_ref[...])
    o_ref[...] = x + _mm(jnp.square(jax.nn.relu(up)), wdown_ref[...])


def _mlp_call(x, g2, wup, wdown, *, tile):
    N, D = x.shape
    FF = wup.shape[1]
    assert N % tile == 0
    row = lambda i: (i, 0)
    const = lambda shape: pl.BlockSpec(shape, lambda i: (0, 0), pipeline_mode=pl.Buffered(1))
    return pl.pallas_call(
        _mlp_kernel,
        grid=(N // tile,),
        in_specs=[pl.BlockSpec((tile, D), row), const((1, D)), const((D, FF)), const((FF, D))],
        out_specs=pl.BlockSpec((tile, D), row),
        out_shape=jax.ShapeDtypeStruct((N, D), F32),
        compiler_params=pltpu.CompilerParams(dimension_semantics=("arbitrary",),
                                             vmem_limit_bytes=VMEM_LIMIT),
        name="mlp",
    )(x, g2, wup, wdown)


def _rope_tables(pos, lanes):
    half = A_HEAD_DIM // 2
    inv = ROPE_THETA ** (-jnp.arange(half, dtype=F32) / half)
    ang = pos.astype(F32)[:, None] * inv[None, :]
    cos, sin = jnp.cos(ang), jnp.sin(ang)
    reps = lanes // A_HEAD_DIM
    return (jnp.tile(jnp.concatenate([cos, cos], axis=1), (1, reps)),
            jnp.tile(jnp.concatenate([-sin, sin], axis=1), (1, reps)))


def _seg_ones(width):
    seg = jnp.arange(width) // A_HEAD_DIM
    return (seg[:, None] == seg[None, :]).astype(BF16)


def _seg_select(width):
    seg = jnp.arange(width) // A_HEAD_DIM
    return (seg[:, None] == jnp.arange(LANES)[None, :]).astype(BF16)


def _paired_gate_weights(wa, wx):
    nb, blk, _ = wa.shape
    per = MXU_DIM // blk
    eye = jnp.eye(per, dtype=wa.dtype)[None, :, None, :, None]

    def block_diag(w):
        return (w.reshape(nb // per, per, blk, 1, blk) * eye).reshape(nb // per, per * blk, per * blk)

    return jnp.concatenate([block_diag(wa), block_diag(wx)], axis=-1).astype(BF16)


def _layer_weights(w):
    (norm1_g, w_in, conv_w, conv_b, lru_wa, lru_ba, lru_wx, lru_bx, lru_lam, m_bi, m_bf, m_norm_g, qn_g, kn_g,
     sinks, w_oa, w_ob, w_oc, b_gate, w_out, norm2_g, w_up, w_down) = w
    D = w_in.shape[0]
    W = conv_w.shape[1]
    MW = m_norm_g.shape[0]
    AW = w_oc.shape[0]
    KW = A_KV_HEADS * A_HEAD_DIM
    sizes = (W, W, MW, MW, MW, MW, M_HEADS, M_HEADS, AW, KW, KW, D, D, D)
    offs = [0]
    for s_ in sizes:
        offs.append(offs[-1] + s_)
    col = lambda i: w_in[:, offs[i]:offs[i + 1]].astype(BF16)
    row = lambda v: v.reshape(1, -1)

    lru_w = [col(0), col(1), col(11), conv_w, row(conv_b), _paired_gate_weights(lru_wa, lru_wx), row(lru_ba),
             row(lru_bx), row(lru_lam), row(b_gate[0]), w_oa.astype(BF16)]
    wif = jnp.zeros((D, LANES), BF16).at[:, :2 * M_HEADS].set(w_in[:, offs[6]:offs[8]].astype(BF16))
    bif = jnp.zeros((1, LANES), F32).at[0, :2 * M_HEADS].set(jnp.concatenate([m_bi, m_bf]))
    mlstm_w = [col(2), col(3), col(4), col(5), wif, bif, row(m_norm_g), col(12), row(b_gate[1]),
               w_ob.astype(BF16)]
    attn_w = [col(8), w_in[:, offs[9]:offs[11]].astype(BF16), col(13), row(jnp.tile(qn_g, AW // A_HEAD_DIM)),
              row(jnp.tile(kn_g, A_KV_HEADS)), _seg_select(AW), _seg_ones(KW), None, None, row(b_gate[2]),
              w_oc.astype(BF16)]
    return (sinks, row(norm1_g), lru_w, mlstm_w, attn_w, w_out.astype(BF16),
            row(norm2_g), w_up.astype(BF16), w_down.astype(BF16))


def _layer(x, state, lw, cos, sin, *, tile, valid):
    sinks, g1, lru_w, mlstm_w, attn_w, wout, g2, wup, wdown = lw
    B, S, D = x.shape
    attn_w = attn_w[:ATTN_COS] + [cos, sin] + attn_w[ATTN_SIN + 1:]
    if S < tile:
        x = jnp.pad(x, ((0, 0), (0, tile - S), (0, 0)))
    outs = _mixer_call(sinks, x, g1, lru_w, mlstm_w, attn_w, wout, state, tile=tile, valid=valid)
    x1, new_state = outs[0][:, :S], outs[1:]
    N = B * S
    x2 = _mlp_call(x1.reshape(N, D), g2, wup, wdown, tile=min(MLP_TILE, N)).reshape(B, S, D)
    return x2, new_state


def kernel(x_prompt, x_sample, state_conv, state_lru, state_mlstm_C, state_mlstm_n, state_mlstm_m, cache_k, cache_v, norm1_g, w_in, conv_w, conv_b, lru_wa, lru_ba, lru_wx, lru_bx, lru_lam, m_bi, m_bf, m_norm_g, qn_g, kn_g, sinks, w_oa, w_ob, w_oc, b_gate, w_out, norm2_g, w_up, w_down):
    B, S, D = x_prompt.shape
    Bs, Sd, _ = x_sample.shape
    depth = w_in.shape[0]
    W = conv_w.shape[-1]
    KW = A_KV_HEADS * A_HEAD_DIM
    p_tile = min(PROMPT_TILE, S)
    assert S % p_tile == 0 and Sd <= SAMPLE_TILE and cache_k.shape[2] == WINDOW

    cos_p, sin_p = _rope_tables(jnp.arange(S, dtype=jnp.int32), KW)
    cos_s, sin_s = _rope_tables(PAST_LEN + jnp.arange(SAMPLE_TILE, dtype=jnp.int32), KW)

    y_p = x_prompt
    y_s = x_sample
    new_p, new_s = [], []
    for l in range(depth):
        lw = _layer_weights((norm1_g[l], w_in[l], conv_w[l], conv_b[l], lru_wa[l], lru_ba[l], lru_wx[l], lru_bx[l],
                             lru_lam[l], m_bi[l], m_bf[l], m_norm_g[l], qn_g[l], kn_g[l], sinks[l], w_oa[l], w_ob[l],
                             w_oc[l], b_gate[l], w_out[l], norm2_g[l], w_up[l], w_down[l]))
        y_p, sp = _layer(y_p, None, lw, cos_p, sin_p, tile=p_tile, valid=p_tile)
        m0 = jnp.zeros((Bs, 1, LANES), F32).at[:, 0, :M_HEADS].set(state_mlstm_m[l])
        st = (state_conv[l], state_lru[l].reshape(Bs, 1, W), state_mlstm_C[l], state_mlstm_n[l], m0,
              cache_k[l].reshape(Bs, WINDOW, KW), cache_v[l].reshape(Bs, WINDOW, KW))
        y_s, ss = _layer(y_s, st, lw, cos_s, sin_s, tile=SAMPLE_TILE, valid=Sd)
        new_p.append(sp)
        new_s.append(ss)

    def unpack(states):
        conv, h, c, n, m, k, v = [jnp.stack(s) for s in zip(*states)]
        nb = conv.shape[1]
        return (conv, h.reshape(depth, nb, W), c, n, m[:, :, 0, :M_HEADS],
                k.reshape(depth, nb, WINDOW, A_KV_HEADS, A_HEAD_DIM),
                v.reshape(depth, nb, WINDOW, A_KV_HEADS, A_HEAD_DIM))

    return (y_p, y_s, *unpack(new_p), *unpack(new_s))
```

```python
import functools

import jax
import jax.numpy as jnp
from jax import lax
from jax.experimental import pallas as pl
from jax.experimental.pallas import tpu as pltpu

F32 = jnp.float32
BF16 = jnp.bfloat16

EPS = 1e-6
CONV_W = 4
LRU_BLOCKS = 8
LRU_C = 8.0
M_HEADS = 4
A_HEAD_DIM = 64
A_KV_HEADS = 2
ATT_CHUNK = 64
WINDOW = 128
PAST_LEN = 2048
ROPE_THETA = 10000.0

SUBLANES = 8
LANES = 128
MXU_DIM = 256
MLSTM_CHUNK = 256
LRU_ROWS = 32
PROMPT_TILE = 256
SAMPLE_TILE = 128
MLP_TILE = 512
VMEM_LIMIT = 60 * 1024 * 1024


def _mm(a, b):
    return jnp.dot(a.astype(BF16), b.astype(BF16), preferred_element_type=F32)


def _mm_nt(a, b):
    return lax.dot_general(a.astype(BF16), b.astype(BF16), (((1,), (1,)), ((), ())),
                           preferred_element_type=F32)


def _mm_tn(a, b):
    return lax.dot_general(a.astype(BF16), b.astype(BF16), (((0,), (0,)), ((), ())),
                           preferred_element_type=F32)


def _mm_split3(a, b01):
    hi = a.astype(BF16)
    rest = a - hi.astype(F32)
    mid = rest.astype(BF16)
    lo = (rest - mid.astype(F32)).astype(BF16)
    dot = lambda p: jnp.dot(p, b01, preferred_element_type=F32)
    return (dot(hi) + dot(mid)) + dot(lo)


def _cumsum_lanes(x, triu01):
    return _mm_split3(x, triu01)


def _rms(x, g):
    ms = jnp.mean(x * x, axis=-1, keepdims=True)
    return (x * lax.rsqrt(ms + EPS)) * g


def _linear_scan(a, b, h0):
    rows, width = a.shape
    sub = lax.broadcasted_iota(jnp.int32, (SUBLANES, width), 0)
    out = []
    carry = h0
    for g in range(rows // SUBLANES):
        a_g = a[g * SUBLANES:(g + 1) * SUBLANES, :]
        b_g = b[g * SUBLANES:(g + 1) * SUBLANES, :]
        d = 1
        while d < SUBLANES:
            m = sub >= d
            b_g = jnp.where(m, a_g * pltpu.roll(b_g, d, axis=0) + b_g, b_g)
            a_g = jnp.where(m, a_g * pltpu.roll(a_g, d, axis=0), a_g)
            d *= 2
        h_g = b_g + a_g * carry
        out.append(h_g)
        carry = h_g[SUBLANES - 1:SUBLANES, :]
    return jnp.concatenate(out, axis=0)


def _lru_init(conv0_ref, h0_ref, xpad_scr, h_scr):
    pad = SUBLANES
    if conv0_ref is None:
        xpad_scr[0:pad, :] = jnp.zeros((pad, xpad_scr.shape[1]), F32)
        h_scr[...] = jnp.zeros(h_scr.shape, F32)
    else:
        xpad_scr[pad - (CONV_W - 1):pad, :] = conv0_ref[0]
        h_scr[...] = h0_ref[0]


def _lru_section(u, wxa_ref, wga_ref, wgg_ref, convw_ref, convb_ref, wgate_ref, ba_ref, bx_ref, lam_ref,
                 bg_ref, woa_ref, convn_ref, hlast_ref, xpad_scr, h_scr, hg_scr, merge_pre, *, tile, valid):
    pad = SUBLANES
    ntail = CONV_W - 1

    xa = _mm(u, wxa_ref[...])
    xpad_scr[pad:pad + tile, :] = xa
    prev = [xpad_scr[pad - k:pad - k + 1, :] for k in range(1, CONV_W)]
    row = lax.broadcasted_iota(jnp.int32, (SUBLANES, xa.shape[1]), 0)

    def later(z, first):
        rolled = pltpu.roll(z, 1, axis=0)
        head = jnp.where(row == 0, first, rolled[0:SUBLANES, :])
        return jnp.concatenate([head, rolled[SUBLANES:, :]], axis=0)

    acc, acc_prev = None, None
    for j in range(CONV_W):
        w_j = convw_ref[j:j + 1, :]
        term = xa * w_j
        if acc is None:
            acc, acc_prev = term, prev[0] * w_j
        else:
            acc = term + later(acc, acc_prev)
            acc_prev = sum(prev[k] * convw_ref[j - k:j - k + 1, :] for k in range(j + 1)) if j < CONV_W - 1 else None
    xc = acc + convb_ref[...]
    yield

    xcb = xc.astype(BF16)
    pairs = wgate_ref.shape[0]
    gates = [jnp.dot(xcb[:, p * MXU_DIM:(p + 1) * MXU_DIM], wgate_ref[p], preferred_element_type=F32)
             for p in range(pairs)]
    r_pre = jnp.concatenate([g[:, :MXU_DIM] for g in gates], axis=1) + ba_ref[...]
    i_pre = jnp.concatenate([g[:, MXU_DIM:] for g in gates], axis=1) + bx_ref[...]
    yield
    ga = _mm(u, wga_ref[...])
    yield

    carry = h_scr[...]
    sp = -LRU_C * jax.nn.softplus(-lam_ref[...])
    for r0 in range(0, tile, LRU_ROWS):
        rows = slice(r0, r0 + LRU_ROWS)
        if r0 >= valid:
            hg_scr[rows, :] = jnp.zeros((LRU_ROWS, hg_scr.shape[1]), F32)
            continue
        r = jax.nn.sigmoid(r_pre[rows])
        i = jax.nn.sigmoid(i_pre[rows])
        log_a = r * sp
        a = jnp.exp(log_a)
        th = jnp.tanh(log_a)
        uu = jnp.sqrt((-2.0 * th) / (1.0 - th)) * (i * xc[rows])
        h = _linear_scan(a, uu, carry)
        carry = h[LRU_ROWS - 1:LRU_ROWS, :]
        if r0 < valid <= r0 + LRU_ROWS:
            h_last = h[valid - 1 - r0:valid - r0, :]
            h_scr[...] = h_last
            hlast_ref[0] = h_last
        hg_scr[rows, :] = h * jax.nn.gelu(ga[rows])
    yield

    y = _mm(hg_scr[...], woa_ref[...])
    tail = xpad_scr[pad + valid - ntail:pad + valid, :]
    xpad_scr[pad - ntail:pad, :] = tail
    convn_ref[0] = tail
    return jax.nn.sigmoid(_merge_pre(u, wgg_ref, merge_pre, "L") + bg_ref[...]) * y


def _mlstm_init(c0_ref, n0_ref, m0_ref, c_scr, n_scr, m_scr):
    if c0_ref is None:
        c_scr[...] = jnp.zeros(c_scr.shape, F32)
        n_scr[...] = jnp.zeros(n_scr.shape, F32)
        m_scr[...] = jnp.zeros(m_scr.shape, F32)
    else:
        c_scr[...] = c0_ref[0]
        n_scr[...] = n0_ref[0]
        m_scr[...] = m0_ref[0]


def _mlstm_section(u, wq_ref, wk_ref, wv_ref, wo_ref, wif_ref, bif_ref, mg_ref, wgg_ref, bg_ref, wob_ref,
                   c_scr, n_scr, m_scr, q_scr, k_scr, v_scr, hg_scr, merge_pre, *, tile, chunk, valid):
    hd = q_scr.shape[1] // M_HEADS
    blocks = [(c * chunk, min(max(valid - c * chunk, 0), chunk)) for c in range(tile // chunk)]

    gates = _mm(u, wif_ref[...]) + bif_ref[...]
    glane = lax.broadcasted_iota(jnp.int32, gates.shape, 1)
    gates = jnp.where((glane >= M_HEADS) & (glane < 2 * M_HEADS), jax.nn.log_sigmoid(gates), gates)
    yield
    q_scr[...] = _mm(u, wq_ref[...])
    yield
    k_scr[...] = _mm(u, wk_ref[...]) * (hd ** -0.5)
    yield
    v_scr[...] = _mm(u, wv_ref[...])
    yield

    rr = lax.broadcasted_iota(jnp.int32, (chunk, chunk), 0)
    cc = lax.broadcasted_iota(jnp.int32, (chunk, chunk), 1)
    causal = cc <= rr
    triu = (rr <= cc).astype(BF16)
    grow = 2 * SUBLANES
    mlane = lax.broadcasted_iota(jnp.int32, (1, LANES), 1)
    ccol = lax.broadcasted_iota(jnp.int32, (1, chunk), 1)
    crow = lax.broadcasted_iota(jnp.int32, (chunk, 1), 0)

    cums = []
    for r0, vl in blocks:
        g_c = gates[r0:r0 + chunk, :]
        g_rows = g_c.T[0:grow, :]
        b_rows = _cumsum_lanes(g_rows, triu)
        b_all = jnp.concatenate([b_rows, jnp.zeros((LANES - grow, chunk), F32)], axis=0).T
        cums.append((g_c, g_rows, b_rows, b_all))
    yield
    og = None

    for (r0, vl), (g_c, g_rows, b_rows, b_all) in zip(blocks, cums):
        if vl == 0:
            hg_scr[r0:r0 + chunk, :] = jnp.zeros((chunk, hg_scr.shape[1]), F32)
            continue
        m_row = m_scr[...]
        heads = []
        for h in range(M_HEADS):
            cs = slice(h * hd, (h + 1) * hd)
            q_h = q_scr[r0:r0 + chunk, cs]
            k_h = k_scr[r0:r0 + chunk, cs]
            b_col = b_all[:, M_HEADS + h:M_HEADS + h + 1]
            b_row = b_rows[M_HEADS + h:M_HEADS + h + 1, :]
            ig_row = g_rows[h:h + 1, :]
            m_prev = m_row[:, h:h + 1]
            log_d = jnp.where(causal, (b_col - b_row) + ig_row, -jnp.inf)
            inter = b_col + m_prev
            m_t = jnp.maximum(inter, jnp.max(log_d, axis=1, keepdims=True))
            s = _mm_nt(q_h, k_h)
            qc = _mm(q_h, c_scr[h])
            heads.append((q_h, k_h, b_col, b_row, ig_row, m_prev, log_d, inter, m_t, s, qc))
            yield
        if og is None:
            og = jax.nn.sigmoid(_mm(u, wo_ref[...]))
            yield
        for h in range(M_HEADS):
            cs = slice(h * hd, (h + 1) * hd)
            q_h, k_h, b_col, b_row, ig_row, m_prev, log_d, inter, m_t, s, qc = heads[h]
            v_h = v_scr[r0:r0 + chunk, cs]
            ig_col = g_c[:, h:h + 1]
            c_h = c_scr[h]
            n_h = n_scr[h:h + 1, :]
            w = s * jnp.exp(log_d - m_t)
            inter_w = jnp.exp(inter - m_t)
            num = _mm(w, v_h) + inter_w * qc
            den = jnp.sum(w, axis=1, keepdims=True) + inter_w * jnp.sum(q_h * n_h, axis=1, keepdims=True)
            hh = num / jnp.maximum(jnp.abs(den), jnp.exp(-m_t))

            b_last = b_col[vl - 1:vl, :]
            tail_row = (b_last - b_row) + ig_row
            tail_col = (b_last - b_col) + ig_col
            if vl < chunk:
                tail_row = jnp.where(ccol < vl, tail_row, -jnp.inf)
                tail_col = jnp.where(crow < vl, tail_col, -jnp.inf)
            m_new = jnp.maximum(b_last + m_prev, jnp.max(tail_row, axis=1, keepdims=True))
            decay = jnp.exp((b_last + m_prev) - m_new)
            kw = k_h * jnp.exp(tail_col - m_new)
            c_scr[h] = decay * c_h + _mm_tn(kw, v_h)
            n_scr[h:h + 1, :] = decay * n_h + jnp.sum(kw, axis=0, keepdims=True)
            m_row = jnp.where(mlane == h, m_new, m_row)

            hn = _rms(hh, mg_ref[:, cs])
            hg_scr[r0:r0 + chunk, cs] = hn * og[r0:r0 + chunk, cs]
            if h == M_HEADS - 1:
                m_scr[...] = m_row
            yield

    y = _mm(hg_scr[...], wob_ref[...])
    return jax.nn.sigmoid(_merge_pre(u, wgg_ref, merge_pre, "M") + bg_ref[...]) * y


def _seg_sum_sq(x, seg01):
    sq = x * x
    hi = sq.astype(BF16)
    lo = (sq - hi.astype(F32)).astype(BF16)
    return jnp.dot(hi, seg01, preferred_element_type=F32) + jnp.dot(lo, seg01, preferred_element_type=F32)


def _seg_mean_sq(x, seg_ones):
    return _seg_sum_sq(x, seg_ones) * (1.0 / A_HEAD_DIM)


def _rope(x, cos, sin_signed):
    width = x.shape[1]
    half = A_HEAD_DIM // 2
    lane = lax.broadcasted_iota(jnp.int32, x.shape, 1)
    partner = jnp.where((lane % A_HEAD_DIM) < half,
                        pltpu.roll(x, width - half, axis=1),
                        pltpu.roll(x, half, axis=1))
    reps = width // cos.shape[1]
    if reps > 1:
        cos = jnp.concatenate([cos] * reps, axis=1)
        sin_signed = jnp.concatenate([sin_signed] * reps, axis=1)
    return x * cos + partner * sin_signed


def _attn_init(k0_ref, v0_ref, k_scr, v_scr):
    P, kvw = WINDOW, k_scr.shape[1]
    if k0_ref is None:
        k_scr[0:P, :] = jnp.zeros((P, kvw), F32)
        v_scr[0:P, :] = jnp.zeros((P, kvw), F32)
    else:
        k_scr[0:P, :] = k0_ref[0]
        v_scr[0:P, :] = v0_ref[0]


def _attn_section(u, t, sinks_ref, wq_ref, wkv_ref, wgg_ref, qg_ref, kg_ref, segred_ref, segk_ref,
                  cos_ref, sin_ref, bg_ref, woc_ref, kn_ref, vn_ref,
                  k_scr, v_scr, o_scr, merge_pre, *, tile, valid, has_history):
    P = WINDOW
    qc = ATT_CHUNK
    kvw = k_scr.shape[1]
    groups_pairs = (wq_ref.shape[1] // kvw) // A_KV_HEADS

    aq = _mm(u, wq_ref[...])
    yield
    akv = _mm(u, wkv_ref[...])
    ak, av = akv[:, :kvw], akv[:, kvw:]
    yield
    cos = cos_ref[...]
    sin = sin_ref[...]
    kn = ak * lax.rsqrt(_seg_mean_sq(ak, segk_ref[...]) + EPS) * kg_ref[...]
    k_scr[P:P + tile, :] = _rope(kn, cos, sin)
    v_scr[P:P + tile, :] = av
    yield
    ms = _seg_sum_sq(aq, segred_ref[...]) * (1.0 / A_HEAD_DIM)
    yield
    rs = lax.rsqrt(ms + EPS)
    lane = lax.broadcasted_iota(jnp.int32, rs.shape, 1)
    rs_wide = jnp.concatenate(
        [jnp.take_along_axis(rs, 2 * blk + (lane >= A_HEAD_DIM).astype(jnp.int32), axis=1)
         for blk in range(aq.shape[1] // LANES)], axis=1)
    qn = aq * rs_wide * (qg_ref[...] * (A_HEAD_DIM ** -0.5))
    qr = _rope(qn, cos, sin).astype(BF16)
    yield

    lane_kv = lax.broadcasted_iota(jnp.int32, (P + qc, kvw), 1)
    lo_half = lane_kv < A_HEAD_DIM
    key_lane = lax.broadcasted_iota(jnp.int32, (1, 2 * P), 1)
    row4 = lax.broadcasted_iota(jnp.int32, (groups_pairs * qc, 1), 0)
    out_lane = lax.broadcasted_iota(jnp.int32, (1, kvw), 1)
    zpad = jnp.zeros((2 * P - (P + qc), kvw), BF16)

    blocks = [(c * qc, min(max(valid - c * qc, 0), qc)) for c in range(tile // qc)]

    sink_cols = {}
    for kvh in range(A_KV_HEADS):
        for half in range(2):
            col = jnp.zeros((groups_pairs * qc, 1), F32)
            for j in range(groups_pairs):
                col = jnp.where((row4 >= j * qc) & (row4 < (j + 1) * qc),
                                sinks_ref[(kvh * groups_pairs + j) * 2 + half], col)
            sink_cols[kvh, half] = col

    scores = {}
    for r0, own in blocks:
        if own == 0:
            o_scr[r0:r0 + qc, :] = jnp.zeros((qc, o_scr.shape[1]), F32)
            continue
        k_win = k_scr[r0:r0 + P + qc, :]
        k_rot = pltpu.roll(k_win, A_HEAD_DIM, axis=1)
        for kvh in range(A_KV_HEADS):
            k_a, k_b = (k_win, k_rot) if kvh == 0 else (k_rot, k_win)
            k_bd = jnp.concatenate([jnp.where(lo_half, k_a, 0.0).astype(BF16), zpad,
                                    jnp.where(lo_half, 0.0, k_b).astype(BF16), zpad], axis=0)
            pair0 = kvh * groups_pairs
            q_st = jnp.concatenate([qr[r0:r0 + qc, (pair0 + j) * kvw:(pair0 + j + 1) * kvw]
                                    for j in range(groups_pairs)], axis=0)
            scores[r0, kvh] = _mm_nt(q_st, k_bd)
            yield

        if has_history:
            first_valid = 0
        else:
            first_valid = P - jnp.minimum(t * tile + r0, P)
        key_ok = (key_lane >= first_valid) & (key_lane < P + own)
        v_win = v_scr[r0:r0 + P + qc, :]
        v_rot = pltpu.roll(v_win, A_HEAD_DIM, axis=1)
        for kvh in range(A_KV_HEADS):
            v_a, v_b = (v_win, v_rot) if kvh == 0 else (v_rot, v_win)
            v_bd = jnp.concatenate([jnp.where(lo_half, v_a, 0.0).astype(BF16), zpad,
                                    jnp.where(lo_half, 0.0, v_b).astype(BF16), zpad], axis=0)
            pair0 = kvh * groups_pairs
            s = scores[r0, kvh]
            probs, inv = [], []
            for half in range(2):
                s_h = jnp.where(key_ok, s[:, half * 2 * P:(half + 1) * 2 * P], -jnp.inf)
                sink = sink_cols[kvh, half]
                mx =jnp.maximum(jnp.max(s_h, axis=1, keepdims=True), sink)
                p = jnp.exp(s_h - mx)
                probs.append(p.astype(BF16))
                inv.append(1.0 / (jnp.sum(p, axis=1, keepdims=True) + jnp.exp(sink - mx)))
            o = jnp.dot(jnp.concatenate(probs, axis=1), v_bd, preferred_element_type=F32)
            o = o * jnp.where(out_lane < A_HEAD_DIM, inv[0], inv[1])
            for j in range(groups_pairs):
                o_scr[r0:r0 + qc, (pair0 + j) * kvw:(pair0 + j + 1) * kvw] = o[j * qc:(j + 1) * qc, :]
            yield

    y = _mm(o_scr[...], woc_ref[...])

    k_new = k_scr[valid:valid + P, :]
    v_new = v_scr[valid:valid + P, :]
    k_scr[0:P, :] = k_new
    v_scr[0:P, :] = v_new
    kn_ref[0] = k_new
    vn_ref[0] = v_new
    return jax.nn.sigmoid(_merge_pre(u, wgg_ref, merge_pre, "A") + bg_ref[...]) * y


def _merge_pre(u, wgg_ref, merge_pre, key):
    if key not in merge_pre:
        merge_pre[key] = _mm(u, wgg_ref[...])
    return merge_pre[key]


def _merge_gate_section(u, wgg_refs, merge_pre):
    for key, wgg_ref in wgg_refs.items():
        _merge_pre(u, wgg_ref, merge_pre, key)
        yield


N_LRU_W, N_MLSTM_W, N_ATTN_W = 11, 10, 11
ATTN_COS, ATTN_SIN = 7, 8
N_STATE = 7


PIECE_ORDER = (
    "LMAAL"
    "MAGM"
    "AGMM"
    "ALG"
    "MMMMM"
    "AAL"
    "MMAAAAMMAAAA"
    "MAAAALAAA"
)


def _trace_in_order(sections, order):
    results = {}

    def step(key):
        try:
            next(sections[key])
        except StopIteration as done:
            results[key] = done.value

    for key in order:
        if key not in results:
            step(key)
    while len(results) < len(sections):
        for key in sections:
            if key not in results:
                step(key)
    return results


def _mixer_kernel(*refs, tile, chunk, valid, has_history):
    it = iter(refs)
    take = lambda n: [next(it) for _ in range(n)]
    sinks_ref, x_ref, g1_ref = take(3)
    lru_w, mlstm_w, attn_w = take(N_LRU_W), take(N_MLSTM_W), take(N_ATTN_W)
    (wout_ref,) = take(1)
    st_in = take(N_STATE) if has_history else [None] * N_STATE
    (y_ref,) = take(1)
    st_out = take(N_STATE)
    xpad_scr, h_scr, lh_scr, c_scr, n_scr, m_scr, q_scr, k_scr, v_scr, hg_scr, ak_scr, av_scr, o_scr = take(13)

    t = pl.program_id(1)
    nt = pl.num_programs(1)

    @pl.when(t == 0)
    def _():
        _lru_init(st_in[0], st_in[1], xpad_scr, h_scr)
        _mlstm_init(st_in[2], st_in[3], st_in[4], c_scr, n_scr, m_scr)
        _attn_init(st_in[5], st_in[6], ak_scr, av_scr)

    x = x_ref[0]
    u = _rms(x, g1_ref[...]).astype(BF16)
    merge_pre = {}
    y = _trace_in_order({
        "L": _lru_section(u, *lru_w, st_out[0], st_out[1], xpad_scr, h_scr, lh_scr, merge_pre,
                          tile=tile, valid=valid),
        "M": _mlstm_section(u, *mlstm_w, c_scr, n_scr, m_scr, q_scr, k_scr, v_scr, hg_scr, merge_pre,
                            tile=tile, chunk=chunk, valid=valid),
        "A": _attn_section(u, t, sinks_ref, *attn_w, st_out[5], st_out[6], ak_scr, av_scr, o_scr, merge_pre,
                           tile=tile, valid=valid, has_history=has_history),
        "G": _merge_gate_section(u, {"L": lru_w[2], "M": mlstm_w[7], "A": attn_w[2]}, merge_pre)}, PIECE_ORDER)
    y_ref[0] = x + _mm((y["L"] + y["M"]) + y["A"], wout_ref[...])

    @pl.when(t == nt - 1)
    def _():
        st_out[2][0] = c_scr[...]
        st_out[3][0] = n_scr[...]
        st_out[4][0] = m_scr[...]


def _mixer_call(sinks, x, g1, lru_w, mlstm_w, attn_w, wout, state, *, tile, valid):
    B, S, D = x.shape
    W = lru_w[0].shape[1]
    MW = mlstm_w[0].shape[1]
    AW = attn_w[0].shape[1]
    KW = A_KV_HEADS * A_HEAD_DIM
    hd = MW // M_HEADS
    nt = S // tile
    chunk = min(MLSTM_CHUNK, tile)
    has_history = state is not None
    assert nt * tile == S and tile % chunk == 0 and tile % ATT_CHUNK == 0 and (valid == tile or nt == 1)
    assert len(lru_w) == N_LRU_W and len(mlstm_w) == N_MLSTM_W and len(attn_w) == N_ATTN_W

    def const(a):
        return pl.BlockSpec(a.shape, lambda b, t, _n=a.ndim: (0,) * _n, pipeline_mode=pl.Buffered(1))

    def per_batch(shape):
        return pl.BlockSpec((1,) + shape[1:], lambda b, t, _n=len(shape): (b,) + (0,) * (_n - 1))

    cos, sin = attn_w[ATTN_COS], attn_w[ATTN_SIN]
    attn_specs = [const(a) for a in attn_w]
    attn_specs[ATTN_COS] = pl.BlockSpec((tile, KW), lambda b, t: (t, 0))
    attn_specs[ATTN_SIN] = pl.BlockSpec((tile, KW), lambda b, t: (t, 0))
    assert cos.shape == (S, KW) and sin.shape == (S, KW)

    state_shapes = [(B, CONV_W - 1, W), (B, 1, W), (B, M_HEADS, hd, hd), (B, M_HEADS, hd), (B, 1, LANES),
                    (B, WINDOW, KW), (B, WINDOW, KW)]
    in_specs = ([pl.BlockSpec(memory_space=pltpu.SMEM),
                 pl.BlockSpec((1, tile, D), lambda b, t: (b, t, 0)), const(g1)]
                + [const(a) for a in lru_w] + [const(a) for a in mlstm_w] + attn_specs + [const(wout)])
    args = [sinks, x, g1, *lru_w, *mlstm_w, *attn_w, wout]
    if has_history:
        in_specs += [per_batch(s) for s in state_shapes]
        args += list(state)
    kern = functools.partial(_mixer_kernel, tile=tile, chunk=chunk, valid=valid, has_history=has_history)
    return pl.pallas_call(
        kern,
        grid=(B, nt),
        in_specs=in_specs,
        out_specs=[pl.BlockSpec((1, tile, D), lambda b, t: (b, t, 0))] + [per_batch(s) for s in state_shapes],
        out_shape=[jax.ShapeDtypeStruct((B, S, D), F32)] + [jax.ShapeDtypeStruct(s, F32) for s in state_shapes],
        scratch_shapes=[
            pltpu.VMEM((tile + SUBLANES, W), F32), pltpu.VMEM((1, W), F32), pltpu.VMEM((tile, W), F32),
            pltpu.VMEM((M_HEADS, hd, hd), F32), pltpu.VMEM((M_HEADS, hd), F32), pltpu.VMEM((1, LANES), F32),
            pltpu.VMEM((tile, MW), F32), pltpu.VMEM((tile, MW), F32), pltpu.VMEM((tile, MW), F32),
            pltpu.VMEM((tile, MW), F32),
            pltpu.VMEM((WINDOW + tile, KW), F32), pltpu.VMEM((WINDOW + tile, KW), F32),
            pltpu.VMEM((tile, AW), F32),
        ],
        compiler_params=pltpu.CompilerParams(dimension_semantics=("arbitrary", "arbitrary"),
                                             vmem_limit_bytes=VMEM_LIMIT),
        name="mixer",
    )(*args)


def _mlp_kernel(x_ref, g2_ref, wup_ref, wdown_ref, o_ref):
    x = x_ref[...]
    up = _mm(_rms(x, g2_ref[...]), wup_ref[...])
    o_ref[...] = x + _mm(jnp.square(jax.nn.relu(up)), wdown_ref[...])


def _mlp_call(x, g2, wup, wdown, *, tile):
    N, D = x.shape
    FF = wup.shape[1]
    assert N % tile == 0
    row = lambda i: (i, 0)
    const = lambda shape: pl.BlockSpec(shape, lambda i: (0, 0), pipeline_mode=pl.Buffered(1))
    return pl.pallas_call(
        _mlp_kernel,
        grid=(N // tile,),
        in_specs=[pl.BlockSpec((tile, D), row), const((1, D)), const((D, FF)), const((FF, D))],
        out_specs=pl.BlockSpec((tile, D), row),
        out_shape=jax.ShapeDtypeStruct((N, D), F32),
        compiler_params=pltpu.CompilerParams(dimension_semantics=("arbitrary",),
                                             vmem_limit_bytes=VMEM_LIMIT),
        name="mlp",
    )(x, g2, wup, wdown)


def _rope_tables(pos, lanes):
    half = A_HEAD_DIM // 2
    inv = ROPE_THETA ** (-jnp.arange(half, dtype=F32) / half)
    ang = pos.astype(F32)[:, None] * inv[None, :]
    cos, sin = jnp.cos(ang), jnp.sin(ang)
    reps = lanes // A_HEAD_DIM
    return (jnp.tile(jnp.concatenate([cos, cos], axis=1), (1, reps)),
            jnp.tile(jnp.concatenate([-sin, sin], axis=1), (1, reps)))


def _seg_ones(width):
    seg = jnp.arange(width) // A_HEAD_DIM
    return (seg[:, None] == seg[None, :]).astype(BF16)


def _seg_select(width):
    seg = jnp.arange(width) // A_HEAD_DIM
    return (seg[:, None] == jnp.arange(LANES)[None, :]).astype(BF16)


def _paired_gate_weights(wa, wx):
    nb, blk, _ = wa.shape
    per = MXU_DIM // blk
    eye = jnp.eye(per, dtype=wa.dtype)[None, :, None, :, None]

    def block_diag(w):
        return (w.reshape(nb // per, per, blk, 1, blk) * eye).reshape(nb // per, per * blk, per * blk)

    return jnp.concatenate([block_diag(wa), block_diag(wx)], axis=-1).astype(BF16)


def _layer_weights(w):
    (norm1_g, w_in, conv_w, conv_b, lru_wa, lru_ba, lru_wx, lru_bx, lru_lam, m_bi, m_bf, m_norm_g, qn_g, kn_g,
     sinks, w_oa, w_ob, w_oc, b_gate, w_out, norm2_g, w_up, w_down) = w
    D = w_in.shape[0]
    W = conv_w.shape[1]
    MW = m_norm_g.shape[0]
    AW = w_oc.shape[0]
    KW = A_KV_HEADS * A_HEAD_DIM
    sizes = (W, W, MW, MW, MW, MW, M_HEADS, M_HEADS, AW, KW, KW, D, D, D)
    offs = [0]
    for s_ in sizes:
        offs.append(offs[-1] + s_)
    col = lambda i: w_in[:, offs[i]:offs[i + 1]].astype(BF16)
    row = lambda v: v.reshape(1, -1)

    lru_w = [col(0), col(1), col(11), conv_w, row(conv_b), _paired_gate_weights(lru_wa, lru_wx), row(lru_ba),
             row(lru_bx), row(lru_lam), row(b_gate[0]), w_oa.astype(BF16)]
    wif = jnp.zeros((D, LANES), BF16).at[:, :2 * M_HEADS].set(w_in[:, offs[6]:offs[8]].astype(BF16))
    bif = jnp.zeros((1, LANES), F32).at[0, :2 * M_HEADS].set(jnp.concatenate([m_bi, m_bf]))
    mlstm_w = [col(2), col(3), col(4), col(5), wif, bif, row(m_norm_g), col(12), row(b_gate[1]),
               w_ob.astype(BF16)]
    attn_w = [col(8), w_in[:, offs[9]:offs[11]].astype(BF16), col(13), row(jnp.tile(qn_g, AW // A_HEAD_DIM)),
              row(jnp.tile(kn_g, A_KV_HEADS)), _seg_select(AW), _seg_ones(KW), None, None, row(b_gate[2]),
              w_oc.astype(BF16)]
    return (sinks, row(norm1_g), lru_w, mlstm_w, attn_w, w_out.astype(BF16),
            row(norm2_g), w_up.astype(BF16), w_down.astype(BF16))


def _layer(x, state, lw, cos, sin, *, tile, valid):
    sinks, g1, lru_w, mlstm_w, attn_w, wout, g2, wup, wdown = lw
    B, S, D = x.shape
    attn_w = attn_w[:ATTN_COS] + [cos, sin] + attn_w[ATTN_SIN + 1:]
    if S < tile:
        x = jnp.pad(x, ((0, 0), (0, tile - S), (0, 0)))
    outs = _mixer_call(sinks, x, g1, lru_w, mlstm_w, attn_w, wout, state, tile=tile, valid=valid)
    x1, new_state = outs[0][:, :S], outs[1:]
    N = B * S
    x2 = _mlp_call(x1.reshape(N, D), g2, wup, wdown, tile=min(MLP_TILE, N)).reshape(B, S, D)
    return x2, new_state


def kernel(x_prompt, x_sample, state_conv, state_lru, state_mlstm_C, state_mlstm_n, state_mlstm_m, cache_k, cache_v, norm1_g, w_in, conv_w, conv_b, lru_wa, lru_ba, lru_wx, lru_bx, lru_lam, m_bi, m_bf, m_norm_g, qn_g, kn_g, sinks, w_oa, w_ob, w_oc, b_gate, w_out, norm2_g, w_up, w_down):
    B, S, D = x_prompt.shape
    Bs, Sd, _ = x_sample.shape
    depth = w_in.shape[0]
    W = conv_w.shape[-1]
    KW = A_KV_HEADS * A_HEAD_DIM
    p_tile = min(PROMPT_TILE, S)
    assert S % p_tile == 0 and Sd <= SAMPLE_TILE and cache_k.shape[2] == WINDOW

    cos_p, sin_p = _rope_tables(jnp.arange(S, dtype=jnp.int32), KW)
    cos_s, sin_s = _rope_tables(PAST_LEN + jnp.arange(SAMPLE_TILE, dtype=jnp.int32), KW)

    y_p = x_prompt
    y_s = x_sample
    new_p, new_s = [], []
    for l in range(depth):
        lw = _layer_weights((norm1_g[l], w_in[l], conv_w[l], conv_b[l], lru_wa[l], lru_ba[l], lru_wx[l], lru_bx[l],
                             lru_lam[l], m_bi[l], m_bf[l], m_norm_g[l], qn_g[l], kn_g[l], sinks[l], w_oa[l], w_ob[l],
                             w_oc[l], b_gate[l], w_out[l], norm2_g[l], w_up[l], w_down[l]))
        y_p, sp = _layer(y_p, None, lw, cos_p, sin_p, tile=p_tile, valid=p_tile)
        m0 = jnp.zeros((Bs, 1, LANES), F32).at[:, 0, :M_HEADS].set(state_mlstm_m[l])
        st = (state_conv[l], state_lru[l].reshape(Bs, 1, W), state_mlstm_C[l], state_mlstm_n[l], m0,
              cache_k[l].reshape(Bs, WINDOW, KW), cache_v[l].reshape(Bs, WINDOW, KW))
        y_s, ss = _layer(y_s, st, lw, cos_s, sin_s, tile=SAMPLE_TILE, valid=Sd)
        new_p.append(sp)
        new_s.append(ss)

    def unpack(states):
        conv, h, c, n, m, k, v = [jnp.stack(s) for s in zip(*states)]
        nb = conv.shape[1]
        return (conv, h.reshape(depth, nb, W), c, n, m[:, :, 0, :M_HEADS],
                k.reshape(depth, nb, WINDOW, A_KV_HEADS, A_HEAD_DIM),
                v.reshape(depth, nb, WINDOW, A_KV_HEADS, A_HEAD_DIM))

    return (y_p, y_s, *unpack(new_p), *unpack(new_s))
```
